```python
import math
import jax, jax.numpy as jnp
from jax import lax
import numpy as np

D_MODEL = 1024
BATCH = 8
SEQ = 4096
DEPTH = 1
DEC_BATCH = 128
DEC_SEQ = 8
PAST_LEN = 8192
PAGE_SIZE = 128

N_HEADS_A = 8
HEAD_DIM_A = 64
D_ATTN = N_HEADS_A * HEAD_DIM_A
N_HEADS_IDX = 8
D_IDX = 64
TOPK_MAX = 256
Q_BLOCK = 128
N_HEADS_G = 4
DK_G = 64
DV_G = 128
D_GLA_K = N_HEADS_G * DK_G
D_GLA_V = N_HEADS_G * DV_G
GATE_RANK = 16
GATE_TAU = 16.0
GLA_CHUNK = 64
D_FF = ((8 * D_MODEL + 3 * 256 - 1) // (3 * 256)) * 256
EPS = 1e-6
POOL_NUM = 5
POOL_DEN = 4
IN_SPLITS = (D_ATTN, D_ATTN, D_ATTN, N_HEADS_IDX * D_IDX, D_IDX, N_HEADS_IDX,
             D_GLA_K, D_GLA_K, D_GLA_V, GATE_RANK, D_GLA_V, D_MODEL, D_MODEL)
D_IN = sum(IN_SPLITS)

kernel_name = 'dsa_gla_gated_hybrid_step'


def rmsnorm(x, g):
    xf = x.astype(jnp.float32)
    y = xf * lax.rsqrt(jnp.mean(xf * xf, axis=-1, keepdims=True) + EPS)
    return (y * g.astype(jnp.float32)).astype(x.dtype)


def mixer_inputs(x, norm_g, w_in, w_gate_up, b_gate):
    B, T = x.shape[0], x.shape[1]
    h = rmsnorm(x, norm_g)
    points = [int(p) for p in np.cumsum(IN_SPLITS)[:-1]]
    (q, k, v, qi, ki, wi, qg, kg, vg, glr, r, ga, gb) = jnp.split(h @ w_in, points, axis=-1)
    log_a = jax.nn.log_sigmoid((glr @ w_gate_up + b_gate).astype(jnp.float32)) / GATE_TAU

    def heads(a, n):
        return a.reshape(B, T, n, a.shape[-1] // n)

    return (heads(q, N_HEADS_A), heads(k, N_HEADS_A), heads(v, N_HEADS_A),
            heads(qi, N_HEADS_IDX), ki, wi,
            heads(qg, N_HEADS_G), heads(kg, N_HEADS_G), heads(vg, N_HEADS_G), heads(log_a, N_HEADS_G),
            r, ga, gb)


def indexer_scores(qi, wi, ki):
    s = jnp.einsum('bthd,bsd->bths', qi.astype(jnp.float32), ki.astype(jnp.float32)) * D_IDX ** -0.5
    return jnp.einsum('bths,bth->bts', jax.nn.relu(s), wi.astype(jnp.float32) * N_HEADS_IDX ** -0.5)


def dsa_prompt(q, k, v, qi, ki, wi):
    B, S = q.shape[0], q.shape[1]
    n_top = min(TOPK_MAX, S // 4)
    key_pos = jnp.arange(S)
    bidx = jnp.arange(B)[:, None, None]

    def block(j):
        t0 = j * Q_BLOCK
        qb = lax.dynamic_slice_in_dim(q, t0, Q_BLOCK, axis=1)
        qib = lax.dynamic_slice_in_dim(qi, t0, Q_BLOCK, axis=1)
        wib = lax.dynamic_slice_in_dim(wi, t0, Q_BLOCK, axis=1)
        q_pos = t0 + jnp.arange(Q_BLOCK)
        score = indexer_scores(qib, wib, ki)
        score = jnp.where(key_pos[None, None, :] <= q_pos[None, :, None], score, -jnp.inf)
        _, idx = lax.top_k(score, n_top)
        valid = idx <= q_pos[None, :, None]
        k_sel = k[bidx, idx]
        v_sel = v[bidx, idx]
        logits = jnp.einsum('bthd,btkhd->bthk', qb, k_sel).astype(jnp.float32) * HEAD_DIM_A ** -0.5
        logits = jnp.where(valid[:, :, None, :], logits, -jnp.inf)
        p = jax.nn.softmax(logits, axis=-1).astype(v.dtype)
        return jnp.einsum('bthk,btkhd->bthd', p, v_sel)

    out = lax.map(block, jnp.arange(S // Q_BLOCK))
    return out.transpose(1, 0, 2, 3, 4).reshape(B, S, D_ATTN)


def dsa_sample(layer, q, k_new, v_new, qi, ki_new, wi, cache_k, cache_v, cache_idx_k, page_table):
    B, T = q.shape[0], q.shape[1]
    n_pages = page_table.shape[1]
    past = n_pages * PAGE_SIZE
    L = past + T
    n_top = min(TOPK_MAX, L // 4)
    ik_past = cache_idx_k[layer, page_table].reshape(B, past, D_IDX)
    ik_all = jnp.concatenate([ik_past.astype(ki_new.dtype), ki_new], axis=1)
    q_pos = past + jnp.arange(T)
    new_pos = past + jnp.arange(T)
    score = indexer_scores(qi, wi, ik_all)
    score = jnp.where(jnp.arange(L)[None, None, :] <= q_pos[None, :, None], score, -jnp.inf)
    _, idx = lax.top_k(score, n_top)
    in_past = idx < past
    pidx = jnp.minimum(idx, past - 1)
    phys = jnp.take_along_axis(page_table, (pidx // PAGE_SIZE).reshape(B, T * n_top), axis=1).reshape(idx.shape)
    slot = pidx % PAGE_SIZE
    k_sel = cache_k[layer, phys, slot].astype(k_new.dtype)
    v_sel = cache_v[layer, phys, slot].astype(v_new.dtype)
    new_sel = jnp.any(idx[..., None] == new_pos[None, None, None, :], axis=2)
    scale = HEAD_DIM_A ** -0.5
    lp = jnp.einsum('bthd,btkhd->bthk', q, k_sel).astype(jnp.float32) * scale
    lp = jnp.where(in_past[:, :, None, :], lp, -jnp.inf)
    ln = jnp.einsum('bthd,bshd->bths', q, k_new).astype(jnp.float32) * scale
    ln = jnp.where(new_sel[:, :, None, :], ln, -jnp.inf)
    p = jax.nn.softmax(jnp.concatenate([lp, ln], axis=-1), axis=-1).astype(v_new.dtype)
    out = (jnp.einsum('bthk,btkhd->bthd', p[..., :n_top], v_sel)
           + jnp.einsum('bths,bshd->bthd', p[..., n_top:], v_new))
    return out.reshape(B, T, D_ATTN)


def gla_chunked(q, k, v, log_a, s0):
    B, T, H, DK = q.shape
    c = math.gcd(T, GLA_CHUNK)
    n = T // c

    def to_chunks(a):
        return a.astype(jnp.float32).reshape(B, n, c, H, a.shape[-1]).transpose(1, 0, 3, 2, 4)

    qc, kc, vc, ac = to_chunks(q * DK ** -0.5), to_chunks(k), to_chunks(v), to_chunks(log_a)
    tril = jnp.tril(jnp.ones((c, c), dtype=bool))

    def step(s, inp):
        qh, kh, vh, ah = inp
        b = lax.cumsum(ah, axis=2)
        inter = jnp.einsum('bhtd,bhde->bhte', qh * jnp.exp(b), s)
        diff = b[:, :, :, None, :] - b[:, :, None, :, :]
        decay = jnp.exp(jnp.where(tril[None, None, :, :, None], diff, -jnp.inf))
        att = jnp.einsum('bhtd,bhsd,bhtsd->bhts', qh, kh, decay)
        intra = jnp.einsum('bhts,bhse->bhte', att, vh)
        b_end = b[:, :, -1:, :]
        s_new = (jnp.exp(b_end[:, :, 0, :])[..., None] * s
                 + jnp.einsum('bhsd,bhse->bhde', kh * jnp.exp(b_end - b), vh))
        return s_new, inter + intra

    s_fin, o = lax.scan(step, s0, (qc, kc, vc, ac))
    o = o.transpose(1, 0, 3, 2, 4).reshape(B, T, H, v.shape[-1])
    return o, s_fin


def gla_branch(qg, kg, vg, log_a, r, s0, g_norm):
    o, s = gla_chunked(qg, kg, vg, log_a, s0)
    o = rmsnorm(o, g_norm).reshape(o.shape[0], o.shape[1], D_GLA_V)
    return (o * jax.nn.silu(r.astype(jnp.float32))).astype(r.dtype), s


def merge_and_ffn(x, a_o, g_o, ga, gb, w_a, w_b, w_o, norm2, w_g, w_u, w_d):
    mixed = jax.nn.sigmoid(ga) * (a_o @ w_a) + jax.nn.sigmoid(gb) * (g_o.astype(x.dtype) @ w_b)
    x = x + mixed @ w_o
    h = rmsnorm(x, norm2)
    return x + (jax.nn.silu(h @ w_g) * (h @ w_u)) @ w_d


def setup_inputs(seed: int = 0) -> dict:
    key = jax.random.key(seed)
    ks = jax.random.split(key, 24)
    f32 = jnp.float32
    n_pages = PAST_LEN // PAGE_SIZE
    n_used = DEC_BATCH * n_pages
    n_pool = (n_used * POOL_NUM) // POOL_DEN

    def w(k, shape, fan_in):
        return jax.random.normal(k, shape, f32) * fan_in ** -0.5

    def gain(k, shape):
        return 1.0 + 0.05 * jax.random.normal(k, shape, f32)

    page_table = jax.random.permutation(ks[6], n_pool)[:n_used].reshape(DEC_BATCH, n_pages).astype(jnp.int32)
    return {
        'x_prompt': jax.random.normal(ks[0], (BATCH, SEQ, D_MODEL), f32),
        'x_sample': jax.random.normal(ks[1], (DEC_BATCH, DEC_SEQ, D_MODEL), f32),
        'cache_k': jax.random.normal(ks[2], (DEPTH, n_pool, PAGE_SIZE, N_HEADS_A, HEAD_DIM_A), f32),
        'cache_v': jax.random.normal(ks[3], (DEPTH, n_pool, PAGE_SIZE, N_HEADS_A, HEAD_DIM_A), f32),
        'cache_idx_k': jax.random.normal(ks[4], (DEPTH, n_pool, PAGE_SIZE, D_IDX), f32),
        'state_gla': jax.random.normal(ks[5], (DEPTH, DEC_BATCH, N_HEADS_G, DK_G, DV_G), f32),
        'page_table': page_table,
        'norm1': gain(ks[7], (DEPTH, D_MODEL)),
        'w_in': w(ks[8], (DEPTH, D_MODEL, D_IN), D_MODEL),
        'w_gate_up': w(ks[9], (DEPTH, GATE_RANK, D_GLA_K), GATE_RANK),
        'b_gate': 0.1 * jax.random.normal(ks[10], (DEPTH, D_GLA_K), f32),
        'g_gla_norm': gain(ks[11], (DEPTH, DV_G)),
        'w_branch_a': w(ks[12], (DEPTH, D_ATTN, D_MODEL), D_ATTN),
        'w_branch_b': w(ks[13], (DEPTH, D_GLA_V, D_MODEL), D_GLA_V),
        'w_out': w(ks[14], (DEPTH, D_MODEL, D_MODEL), D_MODEL),
        'norm2': gain(ks[15], (DEPTH, D_MODEL)),
        'w_ffn_gate': w(ks[16], (DEPTH, D_MODEL, D_FF), D_MODEL),
        'w_ffn_up': w(ks[17], (DEPTH, D_MODEL, D_FF), D_MODEL),
        'w_ffn_down': w(ks[18], (DEPTH, D_FF, D_MODEL), D_FF),
        'norm_final': gain(ks[19], (D_MODEL,)),
    }


def reference(x_prompt, x_sample, cache_k, cache_v, cache_idx_k, state_gla, page_table,
              norm1, w_in, w_gate_up, b_gate, g_gla_norm, w_branch_a, w_branch_b, w_out,
              norm2, w_ffn_gate, w_ffn_up, w_ffn_down, norm_final):
    xp, xs = x_prompt, x_sample
    kp_rows, vp_rows, ikp_rows, sp_states = [], [], [], []
    ks_rows, vs_rows, iks_rows, ss_states = [], [], [], []
    for l in range(DEPTH):
        (q, k, v, qi, ki, wi, qg, kg, vg, la, r, ga, gb) = mixer_inputs(xp, norm1[l], w_in[l], w_gate_up[l], b_gate[l])
        a_o = dsa_prompt(q, k, v, qi, ki, wi)
        s0 = jnp.zeros((xp.shape[0], N_HEADS_G, DK_G, DV_G), jnp.float32)
        g_o, s_fin = gla_branch(qg, kg, vg, la, r, s0, g_gla_norm[l])
        xp = merge_and_ffn(xp, a_o, g_o, ga, gb, w_branch_a[l], w_branch_b[l], w_out[l],
                           norm2[l], w_ffn_gate[l], w_ffn_up[l], w_ffn_down[l])
        kp_rows.append(k)
        vp_rows.append(v)
        ikp_rows.append(ki)
        sp_states.append(s_fin.astype(x_prompt.dtype))
        (q, k, v, qi, ki, wi, qg, kg, vg, la, r, ga, gb) = mixer_inputs(xs, norm1[l], w_in[l], w_gate_up[l], b_gate[l])
        a_o = dsa_sample(l, q, k, v, qi, ki, wi, cache_k, cache_v, cache_idx_k, page_table)
        g_o, s_new = gla_branch(qg, kg, vg, la, r, state_gla[l].astype(jnp.float32), g_gla_norm[l])
        xs = merge_and_ffn(xs, a_o, g_o, ga, gb, w_branch_a[l], w_branch_b[l], w_out[l],
                           norm2[l], w_ffn_gate[l], w_ffn_up[l], w_ffn_down[l])
        ks_rows.append(k)
        vs_rows.append(v)
        iks_rows.append(ki)
        ss_states.append(s_new.astype(x_sample.dtype))
    y_prompt = rmsnorm(xp, norm_final)
    y_sample = rmsnorm(xs, norm_final)
    return (y_prompt, y_sample,
            jnp.stack(kp_rows), jnp.stack(vp_rows), jnp.stack(ikp_rows), jnp.stack(sp_states),
            jnp.stack(ks_rows), jnp.stack(vs_rows), jnp.stack(iks_rows), jnp.stack(ss_states))
```

```python
import functools

import jax
import jax.numpy as jnp
from jax import lax
from jax.experimental import pallas as pl
from jax.experimental.pallas import tpu as pltpu

F32 = jnp.float32
BF16 = jnp.bfloat16

D_ATTN = 512
N_HEADS_A = 8
HEAD_DIM_A = 64
N_HEADS_IDX = 8
D_IDX = 64
TOPK_MAX = 256
N_HEADS_G = 4
DK_G = 64
DV_G = 128
D_GLA_K = N_HEADS_G * DK_G
D_GLA_V = N_HEADS_G * DV_G
GATE_RANK = 16
GATE_TAU = 16.0
GLA_CHUNK = 64
EPS = 1e-6
PAGE_SIZE = 128

LANES = 128
NEG = -1e30
LOWEST = -3.0e38
MAX_BISECT = 64
VMEM_LIMIT = 56 * 1024 * 1024


def _dot(a, b):
    return jnp.dot(a, b, preferred_element_type=F32)


def _dot_nt(a, b):
    return lax.dot_general(a, b, (((1,), (1,)), ((), ())), preferred_element_type=F32)


def _dot_tn(a, b):
    return lax.dot_general(a, b, (((0,), (0,)), ((), ())), preferred_element_type=F32)


def _params(*sem):
    return pltpu.CompilerParams(dimension_semantics=sem, vmem_limit_bytes=VMEM_LIMIT)


def _const_spec(shape):
    n = len(shape)
    return pl.BlockSpec(shape, lambda *_: (0,) * n)


def _inproj_kernel(x_ref, g_ref, wa_ref, wkk_ref, wkw_ref, wc_ref, wd_ref, we_ref, wgu_ref, bg_ref,
                   q_ref, k_ref, v_ref, kb_ref, vb_ref, qi_ref, kk_ref, kw_ref,
                   qkg_ref, vg_ref, la_ref, r_ref, ga_ref, gb_ref):
    x = x_ref[...]
    h = (x * lax.rsqrt(jnp.mean(x * x, axis=-1, keepdims=True) + EPS) * g_ref[...]).astype(BF16)
    a = _dot(h, wa_ref[...])
    q_ref[...] = (a[:, 0:512] * HEAD_DIM_A ** -0.5).astype(BF16)
    k = a[:, 512:1024]
    v = a[:, 1024:1536]
    k_ref[...] = k
    v_ref[...] = v
    kb_ref[...] = k.astype(BF16)
    vb_ref[...] = v.astype(BF16)
    qi_ref[...] = a[:, 1536:2048].astype(BF16)
    kk_ref[...] = _dot(h, wkk_ref[...]).astype(BF16)
    kw_ref[...] = _dot(h, wkw_ref[...])
    c = _dot(h, wc_ref[...])
    qkg_ref[...] = c[:, 0:512]
    vg_ref[...] = c[:, 512:1024]
    glr = _dot(h, wd_ref[...]).astype(BF16)
    z = _dot(glr, wgu_ref[...]) + bg_ref[...]
    la_ref[...] = (jnp.minimum(z, 0.0) - jnp.log1p(jnp.exp(-jnp.abs(z)))) * (1.0 / GATE_TAU)
    e = _dot(h, we_ref[...])
    r_ref[...] = e[:, 0:512]
    ga_ref[...] = e[:, 512:1536]
    gb_ref[...] = e[:, 1536:2560]


def _inproj(x2d, norm_g, w_in, w_gate_up, b_gate, tm):
    n, d = x2d.shape
    wb = w_in.astype(BF16)
    wa = wb[:, 0:2048]
    wki = wb[:, 2048:2112]
    wwi = wb[:, 2112:2120]
    wkk = jnp.concatenate([wki, wki], axis=1)
    wkw = jnp.concatenate([wki, wwi, jnp.zeros((d, LANES - 72), BF16)], axis=1)
    wc = wb[:, 2120:3144]
    wd = jnp.concatenate([wb[:, 3144:3160], jnp.zeros((d, LANES - GATE_RANK), BF16)], axis=1)
    we = wb[:, 3160:5720]
    wgu = jnp.concatenate([w_gate_up.astype(BF16), jnp.zeros((LANES - GATE_RANK, D_GLA_K), BF16)], axis=0)
    weights = (wa, wkk, wkw, wc, wd, we, wgu)
    row = lambda w: pl.BlockSpec((tm, w), lambda i: (i, 0))
    outs = [(512, BF16), (512, F32), (512, F32), (512, BF16), (512, BF16), (512, BF16), (LANES, BF16),
            (LANES, F32), (512, F32), (512, F32), (256, F32), (512, F32), (1024, F32), (1024, F32)]
    return pl.pallas_call(
        _inproj_kernel,
        grid=(n // tm,),
        in_specs=[row(d), _const_spec((1, d))] + [_const_spec(w.shape) for w in weights]
        + [_const_spec((1, D_GLA_K))],
        out_specs=[row(w) for w, _ in outs],
        out_shape=[jax.ShapeDtypeStruct((n, w), dt) for w, dt in outs],
        compiler_params=_params("parallel"),
        name="inproj",
    )(x2d, norm_g.reshape(1, d), *weights, b_gate.reshape(1, D_GLA_K))


def _gla_kernel(qk_ref, v_ref, la_ref, r_ref, gn_ref, s0_ref, go_ref, sfin_ref, st_ref, *, chunk, n_chunks):
    step = pl.program_id(1)

    @pl.when(step == 0)
    def _():
        st_ref[...] = s0_ref[...]

    ri = lax.broadcasted_iota(jnp.int32, (chunk, chunk), 0)
    ci = lax.broadcasted_iota(jnp.int32, (chunk, chunk), 1)
    tril = ci <= ri
    tri = jnp.where(tril, 1.0, 0.0).astype(F32)
    lane_head = lax.broadcasted_iota(jnp.int32, (1, D_GLA_K), 1) // DK_G
    mid = chunk // 2 - 1
    gn = gn_ref[...]

    def body(c, carry):
        r0 = pl.multiple_of(c * chunk, chunk)
        qk = qk_ref[pl.ds(r0, chunk), :]
        q = qk[:, 0:D_GLA_K] * DK_G ** -0.5
        k = qk[:, D_GLA_K:2 * D_GLA_K]
        la = la_ref[pl.ds(r0, chunk), :]
        b = jnp.dot(tri, la, precision=lax.Precision.HIGHEST, preferred_element_type=F32)
        b_mid = b[mid:mid + 1, :]
        b_end = b[chunk - 1:chunk, :]
        qe = q * jnp.exp(b)
        qt = q * jnp.exp(b - b_mid)
        ktb = (k * jnp.exp(b_mid - b)).astype(BF16)
        kdb = (k * jnp.exp(b_end - b)).astype(BF16)
        st = st_ref[...]
        stb = st.astype(BF16)
        new_st = st * jnp.exp(b_end)
        for h in range(N_HEADS_G):
            hm = lane_head == h
            qt_h = jnp.where(hm, qt, 0.0).astype(BF16)
            qe_h = jnp.where(hm, qe, 0.0).astype(BF16)
            att = jnp.where(tril, _dot_nt(qt_h, ktb), 0.0)
            vb = v_ref[pl.ds(r0, chunk), h * DV_G:(h + 1) * DV_G].astype(BF16)
            o = _dot_nt(qe_h, stb) + _dot(att.astype(BF16), vb)
            new_st = new_st + jnp.where(hm, _dot_tn(vb, kdb), 0.0)
            on = o * lax.rsqrt(jnp.mean(o * o, axis=-1, keepdims=True) + EPS) * gn
            rr = r_ref[pl.ds(r0, chunk), h * DV_G:(h + 1) * DV_G]
            go_ref[pl.ds(r0, chunk), h * DV_G:(h + 1) * DV_G] = (on * (rr * jax.nn.sigmoid(rr))).astype(BF16)
        st_ref[...] = new_st
        return carry

    lax.fori_loop(0, n_chunks, body, 0)

    @pl.when(step == pl.num_programs(1) - 1)
    def _():
        sfin_ref[...] = st_ref[...]


def _gla(qkg, vg, la, r, g_norm, s0_t, batch, t_len, chunk, rows_per_step):
    n_steps = t_len // rows_per_step
    row = lambda w: pl.BlockSpec((rows_per_step, w), lambda b, s: (b * n_steps + s, 0))
    st_spec = pl.BlockSpec((None, DV_G, D_GLA_K), lambda b, s: (b, 0, 0))
    return pl.pallas_call(
        functools.partial(_gla_kernel, chunk=chunk, n_chunks=rows_per_step // chunk),
        grid=(batch, n_steps),
        in_specs=[row(512), row(512), row(256), row(512), _const_spec((1, DV_G)), st_spec],
        out_specs=[row(512), st_spec],
        out_shape=[jax.ShapeDtypeStruct((batch * t_len, D_GLA_V), BF16),
                   jax.ShapeDtypeStruct((batch, DV_G, D_GLA_K), F32)],
        scratch_shapes=[pltpu.VMEM((DV_G, D_GLA_K), F32)],
        compiler_params=_params("parallel", "arbitrary"),
        name="gla",
    )(qkg, vg, la, r, g_norm.reshape(1, DV_G), s0_t)


def _state_to_t(s):
    b = s.shape[0]
    return s.transpose(0, 3, 1, 2).reshape(b, DV_G, D_GLA_K)


def _state_from_t(st):
    b = st.shape[0]
    return st.reshape(b, DV_G, N_HEADS_G, DK_G).transpose(0, 2, 3, 1)


def _fold(x, op):
    out = x[:, 0:LANES]
    for i in range(1, x.shape[1] // LANES):
        out = op(out, x[:, i * LANES:(i + 1) * LANES])
    return out


def _topk_threshold(get_tile, set_tile, n_tiles, rows, width, ktop, lo_ref, hi_ref, flo_ref):
    kf = float(ktop)
    one = lambda m: jnp.where(m, 1.0, 0.0)
    zeros = jnp.zeros((rows, LANES), F32)
    rsum = lambda a: jnp.sum(a, axis=1, keepdims=True)

    def stats(c, st):
        mx, mn, nv, c0, c0e = st
        blk = get_tile(c)
        valid = blk > -jnp.inf
        mx = jnp.maximum(mx, _fold(blk, jnp.maximum))
        mn = jnp.minimum(mn, _fold(jnp.where(valid, blk, jnp.inf), jnp.minimum))
        nv = nv + _fold(one(valid), jnp.add)
        c0 = c0 + _fold(one(blk > 0.0), jnp.add)
        c0e = c0e + _fold(one(blk >= 0.0), jnp.add)
        return mx, mn, nv, c0, c0e

    mx, mn, nv, c0, c0e = lax.fori_loop(
        0, n_tiles, stats,
        (jnp.full((rows, LANES), -jnp.inf, F32), jnp.full((rows, LANES), jnp.inf, F32), zeros, zeros, zeros))
    rowmax = jnp.max(mx, axis=1, keepdims=True)
    rowmin = jnp.min(mn, axis=1, keepdims=True)
    nvalid, cnt0, cnt0e = rsum(nv), rsum(c0), rsum(c0e)

    def count_gt(thr):
        def body(c, acc):
            return acc + _fold(one(get_tile(c) > thr), jnp.add)
        return rsum(lax.fori_loop(0, n_tiles, body, zeros))

    small = nvalid <= kf
    ztie = jnp.logical_and(jnp.logical_and(cnt0 < kf, cnt0e >= kf), jnp.logical_not(small))
    frozen = jnp.logical_or(small, ztie)
    pos = cnt0 >= kf
    lo_ref[...] = jnp.where(frozen, 0.0, jnp.where(pos, 0.0, jnp.minimum(rowmin - 1.0, rowmin * 2.0)))
    hi_ref[...] = jnp.where(frozen, 0.0, jnp.where(pos, rowmax, 0.0))
    flo_ref[...] = jnp.where(frozen, kf, jnp.where(pos, cnt0, nvalid))

    def cond(st):
        it, pending = st
        return jnp.logical_and(it < MAX_BISECT, pending > 0)

    def bisect(st):
        it, _ = st
        lo, hi = lo_ref[...], hi_ref[...]
        mid = 0.5 * (lo + hi)
        c = count_gt(mid)
        ge = c >= kf
        flo = jnp.where(ge, c, flo_ref[...])
        lo_ref[...] = jnp.where(ge, mid, lo)
        hi_ref[...] = jnp.where(ge, hi, mid)
        flo_ref[...] = flo
        return it + 1, jnp.max(one(flo != kf)).astype(jnp.int32)

    pending0 = jnp.max(one(flo_ref[...] != kf)).astype(jnp.int32)
    lax.while_loop(cond, bisect, (jnp.int32(0), pending0))

    lo = lo_ref[...]

    def tmin(c, acc):
        blk = get_tile(c)
        return jnp.minimum(acc, _fold(jnp.where(blk > lo, blk, jnp.inf), jnp.minimum))

    thr = jnp.min(lax.fori_loop(0, n_tiles, tmin, jnp.full((rows, LANES), jnp.inf, F32)), axis=1, keepdims=True)
    thr = jnp.where(ztie, 0.0, thr)
    thr = jnp.where(small, LOWEST, thr)

    def gteq(c, st):
        g, e = st
        blk = get_tile(c)
        return g + _fold(one(blk > thr), jnp.add), e + _fold(one(blk == thr), jnp.add)

    g, e = lax.fori_loop(0, n_tiles, gteq, (zeros, zeros))
    need = kf - rsum(g)
    any_tie = jnp.max(one(rsum(e) > need)) > 0.0

    @pl.when(any_tie)
    def _():
        ri = lax.broadcasted_iota(jnp.int32, (width, width), 0)
        ci = lax.broadcasted_iota(jnp.int32, (width, width), 1)
        upper = jnp.where(ri <= ci, 1.0, 0.0).astype(BF16)

        def drop(c, seen):
            blk = get_tile(c)
            eq = blk == thr
            eqf = one(eq)
            rank = _dot(eqf.astype(BF16), upper) + seen - 1.0
            set_tile(c, jnp.where(jnp.logical_and(eq, rank >= need), -jnp.inf, blk))
            return seen + rsum(eqf)

        lax.fori_loop(0, n_tiles, drop, jnp.zeros((rows, 1), F32))

    return thr


def _head_masked(x, n_heads, head_dim):
    per = LANES // head_dim
    half = lax.broadcasted_iota(jnp.int32, (1, LANES), 1) // head_dim
    out = []
    for h in range(n_heads):
        slab = x[:, (h // per) * LANES:(h // per + 1) * LANES]
        out.append(jnp.where(half == h % per, slab, jnp.zeros_like(slab)))
    return out


def _dsa_prompt_kernel(q_ref, qi_ref, kw_ref, k_ref, v_ref, kk_ref, o_ref,
                       sc_ref, lo_ref, hi_ref, flo_ref, acc_ref, *, tq, ktop):
    j = pl.program_id(1)
    n_tiles = j + 1
    t0 = j * tq
    qpos = t0 + lax.broadcasted_iota(jnp.int32, (tq, 1), 0)
    kiota = lax.broadcasted_iota(jnp.int32, (1, tq), 1)

    qih = _head_masked(qi_ref[...], N_HEADS_IDX, D_IDX)
    kw = kw_ref[...]
    wcol = [kw[:, D_IDX + h:D_IDX + h + 1] * (N_HEADS_IDX ** -0.5 * D_IDX ** -0.5) for h in range(N_HEADS_IDX)]

    def score_tile(c, carry):
        k0 = pl.multiple_of(c * tq, tq)
        kk = kk_ref[pl.ds(k0, tq), :]
        acc = jnp.zeros((tq, tq), F32)
        for h in range(N_HEADS_IDX):
            acc = acc + jnp.maximum(_dot_nt(qih[h], kk), 0.0) * wcol[h]
        sc_ref[c] = jnp.where(k0 + kiota <= qpos, acc, -jnp.inf)
        return carry

    lax.fori_loop(0, n_tiles, score_tile, 0)

    def get_tile(c):
        return sc_ref[c]

    def set_tile(c, val):
        sc_ref[c] = val

    thr = _topk_threshold(get_tile, set_tile, n_tiles, tq, tq, ktop, lo_ref, hi_ref, flo_ref)

    qh = _head_masked(q_ref[...], N_HEADS_A, HEAD_DIM_A)
    per = LANES // HEAD_DIM_A
    half = lax.broadcasted_iota(jnp.int32, (1, LANES), 1) // HEAD_DIM_A
    for h in range(N_HEADS_A):
        pair = h // per
        acc_ref[...] = jnp.zeros((tq, LANES), F32)

        def att_tile(c, st, h=h, pair=pair):
            m, l = st
            k0 = pl.multiple_of(c * tq, tq)
            kp = k_ref[pl.ds(k0, tq), pair * LANES:(pair + 1) * LANES]
            vp = v_ref[pl.ds(k0, tq), pair * LANES:(pair + 1) * LANES]
            s = jnp.where(sc_ref[c] >= thr, _dot_nt(qh[h], kp), NEG)
            m_new = jnp.maximum(m, jnp.max(s, axis=1, keepdims=True))
            alpha = jnp.exp(m - m_new)
            p = jnp.exp(s - m_new)
            acc_ref[...] = alpha * acc_ref[...] + _dot(p.astype(BF16), vp)
            return m_new, alpha * l + jnp.sum(p, axis=1, keepdims=True)

        m, l = lax.fori_loop(0, n_tiles, att_tile, (jnp.full((tq, 1), NEG, F32), jnp.zeros((tq, 1), F32)))
        res = acc_ref[...] / l
        pair_out = res if h % per == 0 else jnp.where(half == h % per, res, pair_out)
        if h % per == per - 1:
            o_ref[:, pair * LANES:(pair + 1) * LANES] = pair_out.astype(BF16)


def _dsa_prompt(q, qi, kw, kb, vb, kk, batch, seq, tq):
    nq = seq // tq
    ktop = min(TOPK_MAX, seq // 4)
    qrow = lambda w: pl.BlockSpec((tq, w), lambda b, j: (b * nq + j, 0))
    full = lambda w: pl.BlockSpec((seq, w), lambda b, j: (b, 0))
    return pl.pallas_call(
        functools.partial(_dsa_prompt_kernel, tq=tq, ktop=ktop),
        grid=(batch, nq),
        in_specs=[qrow(512), qrow(512), qrow(LANES), full(512), full(512), full(LANES)],
        out_specs=qrow(512),
        out_shape=jax.ShapeDtypeStruct((batch * seq, D_ATTN), BF16),
        scratch_shapes=[pltpu.VMEM((nq, tq, tq), F32), pltpu.VMEM((tq, 1), F32), pltpu.VMEM((tq, 1), F32),
                        pltpu.VMEM((tq, 1), F32), pltpu.VMEM((tq, LANES), F32)],
        compiler_params=_params("parallel", "arbitrary"),
        name="dsa_prompt",
    )(q, qi, kw, kb, vb, kk)


def _sample_scores_kernel(pt_ref, a_ref, w_ref, kin_ref, *rest, pages):
    page_refs = rest[:pages]
    sp_ref, sn_ref = rest[pages], rest[pages + 1]
    t_new = sn_ref.shape[0]
    a = a_ref[...]
    w = w_ref[...]

    def head_sum(s):
        r = jnp.maximum(s, 0.0) * w
        out = r[0:t_new]
        for h in range(1, N_HEADS_IDX):
            out = out + r[h * t_new:(h + 1) * t_new]
        return out

    for i in range(pages):
        ik = page_refs[i][...].astype(BF16)
        sp_ref[:, i * PAGE_SIZE:(i + 1) * PAGE_SIZE] = head_sum(_dot_nt(a, ik))

    @pl.when(pl.program_id(1) == 0)
    def _():
        sn = head_sum(_dot_nt(a, kin_ref[...]))
        ti = lax.broadcasted_iota(jnp.int32, sn.shape, 0)
        si = lax.broadcasted_iota(jnp.int32, sn.shape, 1)
        sn_ref[...] = jnp.where(si <= ti, sn, -jnp.inf)


def _sample_scores(layer, page_table, a_rows, w_rows, ki_new_pad, cache_idx_k, pages):
    depth, n_pool = cache_idx_k.shape[0], cache_idx_k.shape[1]
    batch, n_pages = page_table.shape
    ht = a_rows.shape[1]
    t_new = ht // N_HEADS_IDX
    cache = cache_idx_k.reshape(depth * n_pool, PAGE_SIZE, D_IDX)
    page_spec = lambda i: pl.BlockSpec(
        (None, PAGE_SIZE, D_IDX), lambda b, s, pt: (layer * n_pool + pt[b, s * pages + i], 0, 0))
    grid_spec = pltpu.PrefetchScalarGridSpec(
        num_scalar_prefetch=1,
        grid=(batch, n_pages // pages),
        in_specs=[pl.BlockSpec((None, ht, D_IDX), lambda b, s, pt: (b, 0, 0)),
                  pl.BlockSpec((None, ht, 1), lambda b, s, pt: (b, 0, 0)),
                  pl.BlockSpec((None, LANES, D_IDX), lambda b, s, pt: (b, 0, 0))]
        + [page_spec(i) for i in range(pages)],
        out_specs=[pl.BlockSpec((None, t_new, pages * PAGE_SIZE), lambda b, s, pt: (b, 0, s)),
                   pl.BlockSpec((None, t_new, LANES), lambda b, s, pt: (b, 0, 0))],
    )
    return pl.pallas_call(
        functools.partial(_sample_scores_kernel, pages=pages),
        grid_spec=grid_spec,
        out_shape=[jax.ShapeDtypeStruct((batch, t_new, n_pages * PAGE_SIZE), F32),
                   jax.ShapeDtypeStruct((batch, t_new, LANES), F32)],
        compiler_params=_params("parallel", "arbitrary"),
        name="sample_scores",
    )(page_table, a_rows, w_rows, ki_new_pad, *([cache] * pages))


def _sample_select_kernel(sp_ref, sn_ref, selp_ref, seln_ref, sc_ref, lo_ref, hi_ref, flo_ref, *, width, ktop):
    rows, past = sp_ref.shape
    n_tiles = (past + LANES) // width
    for c in range(n_tiles):
        lo, hi = c * width, (c + 1) * width
        if hi <= past:
            sc_ref[c] = sp_ref[:, lo:hi]
        elif lo == past:
            sc_ref[c] = sn_ref[...]
        else:
            sc_ref[c] = jnp.concatenate([sp_ref[:, lo:past], sn_ref[...]], axis=1)

    def get_tile(c):
        return sc_ref[c]

    def set_tile(c, val):
        sc_ref[c] = val

    thr = _topk_threshold(get_tile, set_tile, n_tiles, rows, width, ktop, lo_ref, hi_ref, flo_ref)
    for c in range(n_tiles):
        lo, hi = c * width, (c + 1) * width
        sel = jnp.where(sc_ref[c] >= thr, 1.0, 0.0)
        if hi <= past:
            selp_ref[:, lo:hi] = sel
        elif lo == past:
            seln_ref[...] = sel
        else:
            selp_ref[:, lo:past] = sel[:, 0:past - lo]
            seln_ref[...] = sel[:, past - lo:]


def _sample_select(sp, sn, rows, width, ktop):
    n, past = sp.shape
    spec = lambda w: pl.BlockSpec((rows, w), lambda i: (i, 0))
    return pl.pallas_call(
        functools.partial(_sample_select_kernel, width=width, ktop=ktop),
        grid=(n // rows,),
        in_specs=[spec(past), spec(LANES)],
        out_specs=[spec(past), spec(LANES)],
        out_shape=[jax.ShapeDtypeStruct((n, past), F32), jax.ShapeDtypeStruct((n, LANES), F32)],
        scratch_shapes=[pltpu.VMEM(((past + LANES) // width, rows, width), F32), pltpu.VMEM((rows, 1), F32),
                        pltpu.VMEM((rows, 1), F32), pltpu.VMEM((rows, 1), F32)],
        compiler_params=_params("parallel"),
        name="sample_select",
    )(sp, sn)


def _sample_attn_kernel(pt_ref, q_ref, selp_ref, seln_ref, kn_ref, vn_ref, *rest, pages):
    k_refs = rest[:pages]
    v_refs = rest[pages:2 * pages]
    o_ref, m_ref, l_ref, acc_ref = rest[2 * pages:]
    step = pl.program_id(1)
    t_new = o_ref.shape[0]
    q = q_ref[...]

    def tile_rows(sel):
        return jnp.concatenate([sel] * N_HEADS_A, axis=0)

    def update(s, pv):
        m_old = m_ref[...]
        m_new = jnp.maximum(m_old, jnp.max(s, axis=1, keepdims=True))
        alpha = jnp.exp(m_old - m_new)
        p = jnp.exp(s - m_new)
        l_ref[...] = alpha * l_ref[...] + jnp.sum(p, axis=1, keepdims=True)
        acc_ref[...] = alpha * acc_ref[...] + pv(p.astype(BF16))
        m_ref[...] = m_new

    @pl.when(step == 0)
    def _():
        m_ref[...] = jnp.full(m_ref.shape, NEG, F32)
        l_ref[...] = jnp.zeros(l_ref.shape, F32)
        acc_ref[...] = jnp.zeros(acc_ref.shape, F32)
        s = jnp.where(tile_rows(seln_ref[...]) > 0.5, _dot_nt(q, kn_ref[...]), NEG)
        update(s, lambda p: _dot(p, vn_ref[...]))

    kb = [k_refs[i][...].astype(BF16) for i in range(pages)]
    s = jnp.concatenate([_dot_nt(q, kb[i]) for i in range(pages)], axis=1)
    s = jnp.where(tile_rows(selp_ref[...]) > 0.5, s, NEG)

    def pv(p):
        out = _dot(p[:, 0:PAGE_SIZE], v_refs[0][...].astype(BF16))
        for i in range(1, pages):
            out = out + _dot(p[:, i * PAGE_SIZE:(i + 1) * PAGE_SIZE], v_refs[i][...].astype(BF16))
        return out

    update(s, pv)

    @pl.when(step == pl.num_programs(1) - 1)
    def _():
        res = acc_ref[...] / l_ref[...]
        lane_head = lax.broadcasted_iota(jnp.int32, (1, D_ATTN), 1) // HEAD_DIM_A
        out = jnp.zeros((t_new, D_ATTN), F32)
        for h in range(N_HEADS_A):
            out = out + jnp.where(lane_head == h, res[h * t_new:(h + 1) * t_new], 0.0)
        o_ref[...] = out.astype(BF16)


def _sample_attn(layer, page_table, q_rows, selp, seln, kn_pad, vn_pad, cache_k, cache_v, pages):
    depth, n_pool = cache_k.shape[0], cache_k.shape[1]
    batch, n_pages = page_table.shape
    ht = q_rows.shape[1]
    t_new = ht // N_HEADS_A
    ck = cache_k.reshape(depth * n_pool, PAGE_SIZE, D_ATTN)
    cv = cache_v.reshape(depth * n_pool, PAGE_SIZE, D_ATTN)
    page_spec = lambda i: pl.BlockSpec(
        (None, PAGE_SIZE, D_ATTN), lambda b, s, pt: (layer * n_pool + pt[b, s * pages + i], 0, 0))
    per_b = lambda r, w: pl.BlockSpec((None, r, w), lambda b, s, pt: (b, 0, 0))
    grid_spec = pltpu.PrefetchScalarGridSpec(
        num_scalar_prefetch=1,
        grid=(batch, n_pages // pages),
        in_specs=[per_b(ht, D_ATTN),
                  pl.BlockSpec((None, t_new, pages * PAGE_SIZE), lambda b, s, pt: (b, 0, s)),
                  per_b(t_new, LANES), per_b(LANES, D_ATTN), per_b(LANES, D_ATTN)]
        + [page_spec(i) for i in range(pages)] * 2,
        out_specs=per_b(t_new, D_ATTN),
        scratch_shapes=[pltpu.VMEM((ht, 1), F32), pltpu.VMEM((ht, 1), F32), pltpu.VMEM((ht, D_ATTN), F32)],
    )
    return pl.pallas_call(
        functools.partial(_sample_attn_kernel, pages=pages),
        grid_spec=grid_spec,
        out_shape=jax.ShapeDtypeStruct((batch, t_new, D_ATTN), BF16),
        compiler_params=_params("parallel", "arbitrary"),
        name="sample_attn",
    )(page_table, q_rows, selp, seln, kn_pad, vn_pad, *([ck] * pages), *([cv] * pages))


def _merge_kernel(x_ref, a_ref, g_ref, ga_ref, gb_ref, wa_ref, wb_ref, wo_ref, n2_ref, x1_ref, h2_ref):
    mixed = (jax.nn.sigmoid(ga_ref[...]) * _dot(a_ref[...], wa_ref[...])
             + jax.nn.sigmoid(gb_ref[...]) * _dot(g_ref[...], wb_ref[...]))
    x1 = x_ref[...] + _dot(mixed.astype(BF16), wo_ref[...])
    x1_ref[...] = x1
    h2_ref[...] = (x1 * lax.rsqrt(jnp.mean(x1 * x1, axis=-1, keepdims=True) + EPS) * n2_ref[...]).astype(BF16)


def _merge(x2d, a_o, g_o, ga, gb, w_a, w_b, w_o, norm2, tm):
    n, d = x2d.shape
    row = lambda w: pl.BlockSpec((tm, w), lambda i: (i, 0))
    weights = (w_a.astype(BF16), w_b.astype(BF16), w_o.astype(BF16))
    return pl.pallas_call(
        _merge_kernel,
        grid=(n // tm,),
        in_specs=[row(d), row(D_ATTN), row(D_GLA_V), row(d), row(d)]
        + [_const_spec(w.shape) for w in weights] + [_const_spec((1, d))],
        out_specs=[row(d), row(d)],
        out_shape=[jax.ShapeDtypeStruct((n, d), F32), jax.ShapeDtypeStruct((n, d), BF16)],
        compiler_params=_params("parallel"),
        name="merge",
    )(x2d, a_o, g_o, ga, gb, *weights, norm2.reshape(1, d))


def _ffn_kernel(x1_ref, h2_ref, wg_ref, wu_ref, wd_ref, nf_ref, y_ref, *, n_split, final_norm):
    h2 = h2_ref[...]
    y = x1_ref[...]
    fc = wg_ref.shape[1] // n_split
    for i in range(n_split):
        g = _dot(h2, wg_ref[:, i * fc:(i + 1) * fc])
        u = _dot(h2, wu_ref[:, i * fc:(i + 1) * fc])
        act = (g * jax.nn.sigmoid(g) * u).astype(BF16)
        y = y + _dot(act, wd_ref[i * fc:(i + 1) * fc, :])
    if final_norm:
        y = y * lax.rsqrt(jnp.mean(y * y, axis=-1, keepdims=True) + EPS) * nf_ref[...]
    y_ref[...] = y


def _ffn(x1, h2, w_g, w_u, w_d, norm_final, tm, final_norm):
    n, d = x1.shape
    d_ff = w_g.shape[1]
    n_split = 2 if d_ff % (2 * LANES) == 0 else 1
    row = lambda: pl.BlockSpec((tm, d), lambda i: (i, 0))
    weights = (w_g.astype(BF16), w_u.astype(BF16), w_d.astype(BF16))
    return pl.pallas_call(
        functools.partial(_ffn_kernel, n_split=n_split, final_norm=final_norm),
        grid=(n // tm,),
        in_specs=[row(), row()] + [_const_spec(w.shape) for w in weights] + [_const_spec((1, d))],
        out_specs=row(),
        out_shape=jax.ShapeDtypeStruct((n, d), F32),
        compiler_params=_params("parallel"),
        name="ffn",
    )(x1, h2, *weights, norm_final.reshape(1, d))


def _pick(n, prefs):
    for p in prefs:
        if n % p == 0:
            return p
    return n


def kernel(x_prompt, x_sample, cache_k, cache_v, cache_idx_k, state_gla, page_table, norm1, w_in, w_gate_up,
           b_gate, g_gla_norm, w_branch_a, w_branch_b, w_out, norm2, w_ffn_gate, w_ffn_up, w_ffn_down, norm_final):
    depth = w_in.shape[0]
    bp, seq, d = x_prompt.shape
    bs, t_new, _ = x_sample.shape
    n_pages = page_table.shape[1]
    past = n_pages * PAGE_SIZE
    xp = x_prompt.reshape(bp * seq, d)
    xs = x_sample.reshape(bs * t_new, d)
    outs = [[] for _ in range(8)]
    tq = _pick(seq, (256, 128))
    for l in range(depth):
        last = l == depth - 1
        (q, k, v, kb, vb, qi, kk, kw, qkg, vg, la, r, ga, gb) = _inproj(
            xp, norm1[l], w_in[l], w_gate_up[l], b_gate[l], _pick(bp * seq, (256, 128)))
        a_o = _dsa_prompt(q, qi, kw, kb, vb, kk, bp, seq, tq)
        chunk = GLA_CHUNK if seq % GLA_CHUNK == 0 else seq
        g_o, st = _gla(qkg, vg, la, r, g_gla_norm[l], jnp.zeros((bp, DV_G, D_GLA_K), F32), bp, seq, chunk,
                       _pick(seq, (512, 256, 128, 64)))
        tm = _pick(bp * seq, (512, 256, 128))
        x1, h2 = _merge(xp, a_o, g_o, ga, gb, w_branch_a[l], w_branch_b[l], w_out[l], norm2[l], tm)
        xp = _ffn(x1, h2, w_ffn_gate[l], w_ffn_up[l], w_ffn_down[l], norm_final, _pick(bp * seq, (256, 128)), last)
        outs[0].append(k.reshape(bp, seq, N_HEADS_A, HEAD_DIM_A))
        outs[1].append(v.reshape(bp, seq, N_HEADS_A, HEAD_DIM_A))
        outs[2].append(kw[:, 0:D_IDX].reshape(bp, seq, D_IDX))
        outs[3].append(_state_from_t(st))

        (q, k, v, kb, vb, qi, kk, kw, qkg, vg, la, r, ga, gb) = _inproj(
            xs, norm1[l], w_in[l], w_gate_up[l], b_gate[l], _pick(bs * t_new, (256, 128)))
        pages = _pick(n_pages, (8, 4, 2, 1))
        a_rows = qi.reshape(bs, t_new, N_HEADS_IDX, D_IDX).transpose(0, 2, 1, 3).reshape(bs, N_HEADS_IDX * t_new, D_IDX)
        w_rows = (kw[:, D_IDX:D_IDX + N_HEADS_IDX] * (N_HEADS_IDX ** -0.5 * D_IDX ** -0.5)).reshape(
            bs, t_new, N_HEADS_IDX).transpose(0, 2, 1).reshape(bs, N_HEADS_IDX * t_new, 1)
        pad_rows = lambda a: jnp.pad(a.reshape(bs, t_new, a.shape[-1]), ((0, 0), (0, LANES - t_new), (0, 0)))
        sp, sn = _sample_scores(l, page_table, a_rows, w_rows, pad_rows(kk[:, 0:D_IDX]), cache_idx_k, pages)
        ktop = min(TOPK_MAX, (past + t_new) // 4)
        width = 5 * LANES if (past + LANES) % (5 * LANES) == 0 else LANES
        rows = _pick(bs * t_new, (128, 64, 32, 16, 8))
        selp, seln = _sample_select(sp.reshape(bs * t_new, past), sn.reshape(bs * t_new, LANES), rows, width, ktop)
        head_of_lane = jnp.arange(D_ATTN) // HEAD_DIM_A
        q_rows = jnp.where(head_of_lane[None, None, None, :] == jnp.arange(N_HEADS_A)[None, :, None, None],
                           q.reshape(bs, 1, t_new, D_ATTN), jnp.zeros((), BF16)).reshape(bs, N_HEADS_A * t_new, D_ATTN)
        a_o = _sample_attn(l, page_table, q_rows, selp.reshape(bs, t_new, past), seln.reshape(bs, t_new, LANES),
                           pad_rows(kb), pad_rows(vb), cache_k, cache_v, pages)
        gchunk = 16
        pad_g = lambda a: jnp.pad(a.reshape(bs, t_new, a.shape[-1]), ((0, 0), (0, gchunk - t_new), (0, 0))).reshape(
            bs * gchunk, a.shape[-1])
        g_o, st = _gla(pad_g(qkg), pad_g(vg), pad_g(la), pad_g(r), g_gla_norm[l], _state_to_t(state_gla[l].astype(F32)),
                       bs, gchunk, gchunk, gchunk)
        g_o = g_o.reshape(bs, gchunk, D_GLA_V)[:, 0:t_new].reshape(bs * t_new, D_GLA_V)
        tm = _pick(bs * t_new, (512, 256, 128))
        x1, h2 = _merge(xs, a_o.reshape(bs * t_new, D_ATTN), g_o, ga, gb, w_branch_a[l], w_branch_b[l], w_out[l],
                        norm2[l], tm)
        xs = _ffn(x1, h2, w_ffn_gate[l], w_ffn_up[l], w_ffn_down[l], norm_final, _pick(bs * t_new, (256, 128)), last)
        outs[4].append(k.reshape(bs, t_new, N_HEADS_A, HEAD_DIM_A))
        outs[5].append(v.reshape(bs, t_new, N_HEADS_A, HEAD_DIM_A))
        outs[6].append(kw[:, 0:D_IDX].reshape(bs, t_new, D_IDX))
        outs[7].append(_state_from_t(st))
    return (xp.reshape(bp, seq, d), xs.reshape(bs, t_new, d)) + tuple(jnp.stack(o) for o in outs)
```

```python
import functools

import jax
import jax.numpy as jnp
from jax import lax
from jax.experimental import pallas as pl
from jax.experimental.pallas import tpu as pltpu

F32 = jnp.float32
BF16 = jnp.bfloat16

D_ATTN = 512
N_HEADS_A = 8
HEAD_DIM_A = 64
N_HEADS_IDX = 8
D_IDX = 64
TOPK_MAX = 256
N_HEADS_G = 4
DK_G = 64
DV_G = 128
D_GLA_K = N_HEADS_G * DK_G
D_GLA_V = N_HEADS_G * DV_G
GATE_RANK = 16
GATE_TAU = 16.0
GLA_CHUNK = 64
EPS = 1e-6
PAGE_SIZE = 128

LANES = 128
NEG = -1e30
LOWEST = -3.0e38
MAX_BISECT = 64
Q_SCALE = HEAD_DIM_A ** -0.5 * 1.4426950408889634
VMEM_LIMIT = 56 * 1024 * 1024


def _dot(a, b):
    return jnp.dot(a, b, preferred_element_type=F32)


def _dot_nt(a, b):
    return lax.dot_general(a, b, (((1,), (1,)), ((), ())), preferred_element_type=F32)


def _dot_tn(a, b):
    return lax.dot_general(a, b, (((0,), (0,)), ((), ())), preferred_element_type=F32)


def _params(*sem):
    return pltpu.CompilerParams(dimension_semantics=sem, vmem_limit_bytes=VMEM_LIMIT)


def _const_spec(shape):
    n = len(shape)
    return pl.BlockSpec(shape, lambda *_: (0,) * n)


def _chunks(x):
    return [x[:, i * LANES:(i + 1) * LANES] for i in range(x.shape[1] // LANES)]


def _rep(col, rows):
    return jnp.broadcast_to(col, (rows, LANES))


def _inproj_kernel(x_ref, g_ref, wq_ref, wkv_ref, wkk_ref, wkw_ref, wc_ref, wd_ref, we_ref, wgu_ref, bg_ref,
                   q_ref, qi_ref, kw_ref, qkg_ref, vg_ref, la_ref, r_ref, ga_ref, gb_ref, *kv_refs, key_major):
    x = x_ref[...]
    h = (x * lax.rsqrt(jnp.mean(x * x, axis=-1, keepdims=True) + EPS) * g_ref[...]).astype(BF16)
    a = _dot(h, wq_ref[...])
    q_ref[...] = (a[:, 0:512] * Q_SCALE).astype(BF16)
    qi_ref[...] = a[:, 512:1024].astype(BF16)
    kw_ref[...] = _dot(h, wkw_ref[...])
    if key_major:
        kt_ref, vt_ref, ktb_ref, vtb_ref, kit_ref, kkt_ref = kv_refs
        kv = _dot_nt(wkv_ref[...], h)
        kt_ref[...] = kv[0:512]
        vt_ref[...] = kv[512:1024]
        ktb_ref[...] = kv[0:512].astype(BF16)
        vtb_ref[...] = kv[512:1024].astype(BF16)
        kk = _dot_nt(wkk_ref[...], h)
        kit_ref[...] = kk[0:D_IDX]
        kkt_ref[...] = kk.astype(BF16)
    else:
        k_ref, v_ref, kb_ref, vb_ref, kk_ref = kv_refs
        kv = _dot(h, wkv_ref[...])
        k_ref[...] = kv[:, 0:512]
        v_ref[...] = kv[:, 512:1024]
        kb_ref[...] = kv[:, 0:512].astype(BF16)
        vb_ref[...] = kv[:, 512:1024].astype(BF16)
        kk_ref[...] = _dot(h, wkk_ref[...]).astype(BF16)
    c = _dot(h, wc_ref[...])
    qkg_ref[...] = c[:, 0:512]
    vg_ref[...] = c[:, 512:1024]
    glr = _dot(h, wd_ref[...]).astype(BF16)
    z = _dot(glr, wgu_ref[...]) + bg_ref[...]
    la_ref[...] = (jnp.minimum(z, 0.0) - jnp.log1p(jnp.exp(-jnp.abs(z)))) * (1.0 / GATE_TAU)
    e = _dot(h, we_ref[...])
    r_ref[...] = e[:, 0:512]
    ga_ref[...] = e[:, 512:1536]
    gb_ref[...] = e[:, 1536:2560]


def _inproj(x2d, norm_g, w_in, w_gate_up, b_gate, batch, tm, key_major):
    n, d = x2d.shape
    t_len = n // batch
    nt = t_len // tm
    wb = w_in.astype(BF16)
    wq = jnp.concatenate([wb[:, 0:512], wb[:, 1536:2048]], axis=1)
    wkv = wb[:, 512:1536]
    wki = wb[:, 2048:2112]
    wwi = wb[:, 2112:2120]
    wkk = jnp.concatenate([wki, wki], axis=1)
    if key_major:
        wkv, wkk = wkv.T, wkk.T
    wkw = jnp.concatenate([wki, wwi, jnp.zeros((d, LANES - 72), BF16)], axis=1)
    wc = wb[:, 2120:3144]
    wd = jnp.concatenate([wb[:, 3144:3160], jnp.zeros((d, LANES - GATE_RANK), BF16)], axis=1)
    we = wb[:, 3160:5720]
    wgu = jnp.concatenate([w_gate_up.astype(BF16), jnp.zeros((LANES - GATE_RANK, D_GLA_K), BF16)], axis=0)
    weights = (wq, wkv, wkk, wkw, wc, wd, we, wgu)
    row = lambda w: pl.BlockSpec((tm, w), lambda b, i: (b * nt + i, 0))
    col = lambda r: pl.BlockSpec((None, r, tm), lambda b, i: (b, 0, i))
    outs = [(512, BF16), (512, BF16), (LANES, F32), (512, F32), (512, F32), (256, F32), (512, F32),
            (1024, F32), (1024, F32)]
    out_specs = [row(w) for w, _ in outs]
    out_shape = [jax.ShapeDtypeStruct((n, w), dt) for w, dt in outs]
    if key_major:
        kv_outs = [(512, F32), (512, F32), (512, BF16), (512, BF16), (D_IDX, F32), (LANES, BF16)]
        out_specs += [col(r) for r, _ in kv_outs]
        out_shape += [jax.ShapeDtypeStruct((batch, r, t_len), dt) for r, dt in kv_outs]
    else:
        kv_outs = [(512, F32), (512, F32), (512, BF16), (512, BF16), (LANES, BF16)]
        out_specs += [row(w) for w, _ in kv_outs]
        out_shape += [jax.ShapeDtypeStruct((n, w), dt) for w, dt in kv_outs]
    return pl.pallas_call(
        functools.partial(_inproj_kernel, key_major=key_major),
        grid=(batch, nt),
        in_specs=[row(d), _const_spec((1, d))] + [_const_spec(w.shape) for w in weights]
        + [_const_spec((1, D_GLA_K))],
        out_specs=out_specs,
        out_shape=out_shape,
        compiler_params=_params("parallel", "parallel"),
        name="inproj",
    )(x2d, norm_g.reshape(1, d), *weights, b_gate.reshape(1, D_GLA_K))


def _gla_kernel(qk_ref, v_ref, la_ref, r_ref, gn_ref, s0_ref, go_ref, sfin_ref, st_ref, *, chunk, n_chunks):
    step = pl.program_id(1)

    @pl.when(step == 0)
    def _():
        st_ref[...] = s0_ref[...]

    ri = lax.broadcasted_iota(jnp.int32, (chunk, chunk), 0)
    ci = lax.broadcasted_iota(jnp.int32, (chunk, chunk), 1)
    tril = ci <= ri
    tri = jnp.where(tril, 1.0, 0.0).astype(F32)
    lane_head = lax.broadcasted_iota(jnp.int32, (1, D_GLA_K), 1) // DK_G
    mid = chunk // 2 - 1
    gn = gn_ref[...]

    def body(c, carry):
        r0 = pl.multiple_of(c * chunk, chunk)
        qk = qk_ref[pl.ds(r0, chunk), :]
        q = qk[:, 0:D_GLA_K] * DK_G ** -0.5
        k = qk[:, D_GLA_K:2 * D_GLA_K]
        la = la_ref[pl.ds(r0, chunk), :]
        b = jnp.dot(tri, la, precision=lax.Precision.HIGHEST, preferred_element_type=F32)
        b_mid = b[mid:mid + 1, :]
        b_end = b[chunk - 1:chunk, :]
        qe = q * jnp.exp(b)
        qt = q * jnp.exp(b - b_mid)
        ktb = (k * jnp.exp(b_mid - b)).astype(BF16)
        kdb = (k * jnp.exp(b_end - b)).astype(BF16)
        st = st_ref[...]
        stb = st.astype(BF16)
        new_st = st * jnp.exp(b_end)
        for h in range(N_HEADS_G):
            hm = lane_head == h
            qt_h = jnp.where(hm, qt, 0.0).astype(BF16)
            qe_h = jnp.where(hm, qe, 0.0).astype(BF16)
            att = jnp.where(tril, _dot_nt(qt_h, ktb), 0.0)
            vb = v_ref[pl.ds(r0, chunk), h * DV_G:(h + 1) * DV_G].astype(BF16)
            o = _dot_nt(qe_h, stb) + _dot(att.astype(BF16), vb)
            new_st = new_st + jnp.where(hm, _dot_tn(vb, kdb), 0.0)
            on = o * lax.rsqrt(jnp.mean(o * o, axis=-1, keepdims=True) + EPS) * gn
            rr = r_ref[pl.ds(r0, chunk), h * DV_G:(h + 1) * DV_G]
            go_ref[pl.ds(r0, chunk), h * DV_G:(h + 1) * DV_G] = (on * (rr * jax.nn.sigmoid(rr))).astype(BF16)
        st_ref[...] = new_st
        return carry

    lax.fori_loop(0, n_chunks, body, 0)

    @pl.when(step == pl.num_programs(1) - 1)
    def _():
        sfin_ref[...] = st_ref[...]


def _gla(qkg, vg, la, r, g_norm, s0_t, batch, t_len, chunk, rows_per_step):
    n_steps = t_len // rows_per_step
    row = lambda w: pl.BlockSpec((rows_per_step, w), lambda b, s: (b * n_steps + s, 0))
    st_spec = pl.BlockSpec((None, DV_G, D_GLA_K), lambda b, s: (b, 0, 0))
    return pl.pallas_call(
        functools.partial(_gla_kernel, chunk=chunk, n_chunks=rows_per_step // chunk),
        grid=(batch, n_steps),
        in_specs=[row(512), row(512), row(256), row(512), _const_spec((1, DV_G)), st_spec],
        out_specs=[row(512), st_spec],
        out_shape=[jax.ShapeDtypeStruct((batch * t_len, D_GLA_V), BF16),
                   jax.ShapeDtypeStruct((batch, DV_G, D_GLA_K), F32)],
        scratch_shapes=[pltpu.VMEM((DV_G, D_GLA_K), F32)],
        compiler_params=_params("parallel", "arbitrary"),
        name="gla",
    )(qkg, vg, la, r, g_norm.reshape(1, DV_G), s0_t)


def _state_to_t(s):
    b = s.shape[0]
    return s.transpose(0, 3, 1, 2).reshape(b, DV_G, D_GLA_K)


def _state_from_t(st):
    b = st.shape[0]
    return st.reshape(b, DV_G, N_HEADS_G, DK_G).transpose(0, 2, 3, 1)


def _topk_threshold(get_tile, set_tile, n_tiles, rows, width, ktop, lo_ref, hi_ref, flo_ref):
    kf = float(ktop)
    one = lambda m: jnp.where(m, 1.0, 0.0)
    zeros = jnp.zeros((rows, LANES), F32)
    rsum = lambda a: _rep(jnp.sum(a, axis=1, keepdims=True), rows)

    def acc_chunks(blk, acc, fn, op):
        for ch in _chunks(blk):
            acc = op(acc, fn(ch))
        return acc

    def stats(c, st):
        mx, mn, nv, c0, c0e = st
        blk = get_tile(c)
        mx = acc_chunks(blk, mx, lambda ch: ch, jnp.maximum)
        mn = acc_chunks(blk, mn, lambda ch: jnp.where(ch > -jnp.inf, ch, jnp.inf), jnp.minimum)
        nv = acc_chunks(blk, nv, lambda ch: one(ch > -jnp.inf), jnp.add)
        c0 = acc_chunks(blk, c0, lambda ch: one(ch > 0.0), jnp.add)
        c0e = acc_chunks(blk, c0e, lambda ch: one(ch >= 0.0), jnp.add)
        return mx, mn, nv, c0, c0e

    mx, mn, nv, c0, c0e = lax.fori_loop(
        0, n_tiles, stats,
        (jnp.full((rows, LANES), -jnp.inf, F32), jnp.full((rows, LANES), jnp.inf, F32), zeros, zeros, zeros))
    rowmax = _rep(jnp.max(mx, axis=1, keepdims=True), rows)
    rowmin = _rep(jnp.min(mn, axis=1, keepdims=True), rows)
    nvalid, cnt0, cnt0e = rsum(nv), rsum(c0), rsum(c0e)

    def count_gt(thr):
        def body(c, acc):
            return acc_chunks(get_tile(c), acc, lambda ch: one(ch > thr), jnp.add)
        return rsum(lax.fori_loop(0, n_tiles, body, zeros))

    small = nvalid <= kf
    ztie = jnp.logical_and(jnp.logical_and(cnt0 < kf, cnt0e >= kf), jnp.logical_not(small))
    frozen = jnp.logical_or(small, ztie)
    pos = cnt0 >= kf
    lo_ref[...] = jnp.where(frozen, 0.0, jnp.where(pos, 0.0, jnp.minimum(rowmin - 1.0, rowmin * 2.0)))
    hi_ref[...] = jnp.where(frozen, 0.0, jnp.where(pos, rowmax, 0.0))
    flo_ref[...] = jnp.where(frozen, kf, jnp.where(pos, cnt0, nvalid))

    def cond(st):
        it, pending = st
        return jnp.logical_and(it < MAX_BISECT, pending > 0)

    def bisect(st):
        it, _ = st
        lo, hi = lo_ref[...], hi_ref[...]
        mid = 0.5 * (lo + hi)
        c = count_gt(mid)
        ge = c >= kf
        flo = jnp.where(ge, c, flo_ref[...])
        lo_ref[...] = jnp.where(ge, mid, lo)
        hi_ref[...] = jnp.where(ge, hi, mid)
        flo_ref[...] = flo
        return it + 1, jnp.max(one(flo != kf)).astype(jnp.int32)

    pending0 = jnp.max(one(flo_ref[...] != kf)).astype(jnp.int32)
    lax.while_loop(cond, bisect, (jnp.int32(0), pending0))

    lo = lo_ref[...]

    def tmin(c, acc):
        return acc_chunks(get_tile(c), acc, lambda ch: jnp.where(ch > lo, ch, jnp.inf), jnp.minimum)

    thr = _rep(jnp.min(lax.fori_loop(0, n_tiles, tmin, jnp.full((rows, LANES), jnp.inf, F32)),
                       axis=1, keepdims=True), rows)
    thr = jnp.where(ztie, 0.0, thr)
    thr = jnp.where(small, LOWEST, thr)

    def gteq(c, st):
        g, e = st
        blk = get_tile(c)
        return (acc_chunks(blk, g, lambda ch: one(ch > thr), jnp.add),
                acc_chunks(blk, e, lambda ch: one(ch == thr), jnp.add))

    g, e = lax.fori_loop(0, n_tiles, gteq, (zeros, zeros))
    need = kf - rsum(g)
    any_tie = jnp.max(one(rsum(e) > need)) > 0.0

    @pl.when(any_tie)
    def _():
        ri = lax.broadcasted_iota(jnp.int32, (width, width), 0)
        ci = lax.broadcasted_iota(jnp.int32, (width, width), 1)
        upper = jnp.where(ri <= ci, 1.0, 0.0).astype(BF16)
        thr_c, need_c = thr[:, 0:1], need[:, 0:1]

        def drop(c, seen):
            blk = get_tile(c)
            eq = blk == thr_c
            eqf = one(eq)
            rank = _dot(eqf.astype(BF16), upper) + seen - 1.0
            set_tile(c, jnp.where(jnp.logical_and(eq, rank >= need_c), -jnp.inf, blk))
            return seen + jnp.sum(eqf, axis=1, keepdims=True)

        lax.fori_loop(0, n_tiles, drop, jnp.zeros((rows, 1), F32))

    return thr


def _head_masked(x, n_heads, head_dim):
    per = LANES // head_dim
    half = lax.broadcasted_iota(jnp.int32, (1, LANES), 1) // head_dim
    out = []
    for h in range(n_heads):
        slab = x[:, (h // per) * LANES:(h // per + 1) * LANES]
        out.append(jnp.where(half == h % per, slab, jnp.zeros_like(slab)))
    return out


def _dsa_prompt_kernel(q_ref, qi_ref, kw_ref, kt_ref, vt_ref, kkt_ref, o_ref,
                       sc_ref, lo_ref, hi_ref, flo_ref, m_ref, l_ref, acc_ref, *, tq, ktop):
    j = pl.program_id(1)
    n_tiles = j + 1
    t0 = j * tq
    qpos = t0 + lax.broadcasted_iota(jnp.int32, (tq, 1), 0)
    kiota = lax.broadcasted_iota(jnp.int32, (1, tq), 1)
    per = LANES // HEAD_DIM_A

    qih = _head_masked(qi_ref[...], N_HEADS_IDX, D_IDX)
    kw = kw_ref[...]
    wrep = [_rep(kw[:, D_IDX + h:D_IDX + h + 1] * (N_HEADS_IDX ** -0.5 * D_IDX ** -0.5), tq)
            for h in range(N_HEADS_IDX)]

    def score_tile(c, carry):
        k0 = pl.multiple_of(c * tq, tq)
        kk = kkt_ref[:, pl.ds(k0, tq)]
        acc = [jnp.zeros((tq, LANES), F32) for _ in range(tq // LANES)]
        for h in range(N_HEADS_IDX):
            s = _chunks(_dot(qih[h], kk))
            acc = [a + jnp.maximum(x, 0.0) * wrep[h] for a, x in zip(acc, s)]
        sc_ref[c] = jnp.where(k0 + kiota <= qpos, jnp.concatenate(acc, axis=1), -jnp.inf)
        return carry

    lax.fori_loop(0, n_tiles, score_tile, 0)

    def get_tile(c):
        return sc_ref[c]

    def set_tile(c, val):
        sc_ref[c] = val

    thr = _topk_threshold(get_tile, set_tile, n_tiles, tq, tq, ktop, lo_ref, hi_ref, flo_ref)

    qh = _head_masked(q_ref[...], N_HEADS_A, HEAD_DIM_A)
    m_ref[...] = jnp.full(m_ref.shape, NEG, F32)
    l_ref[...] = jnp.zeros(l_ref.shape, F32)
    acc_ref[...] = jnp.zeros(acc_ref.shape, F32)

    def att_tile(c, carry):
        k0 = pl.multiple_of(c * tq, tq)
        bias = [jnp.where(ch >= thr, 0.0, NEG) for ch in _chunks(sc_ref[c])]
        for h in range(N_HEADS_A):
            rows = slice((h // per) * LANES, (h // per + 1) * LANES)
            kp = kt_ref[rows, pl.ds(k0, tq)]
            vp = vt_ref[rows, pl.ds(k0, tq)]
            s = [x + b for b, x in zip(bias, _chunks(_dot(qh[h], kp)))]
            m_old = m_ref[h]
            smax = s[0]
            for x in s[1:]:
                smax = jnp.maximum(smax, x)
            m_new = jnp.maximum(m_old, _rep(jnp.max(smax, axis=1, keepdims=True), tq))
            alpha = jnp.exp2(m_old - m_new)
            p = [jnp.exp2(x - m_new) for x in s]
            psum = p[0]
            for x in p[1:]:
                psum = psum + x
            l_ref[h] = alpha * l_ref[h] + _rep(jnp.sum(psum, axis=1, keepdims=True), tq)
            pv = _dot_nt(jnp.concatenate(p, axis=1).astype(BF16), vp)
            acc_ref[h] = alpha * acc_ref[h] + pv
            m_ref[h] = m_new
        return carry

    lax.fori_loop(0, n_tiles, att_tile, 0)

    half = lax.broadcasted_iota(jnp.int32, (1, LANES), 1) // HEAD_DIM_A
    for pair in range(N_HEADS_A // per):
        out = acc_ref[pair * per] / l_ref[pair * per]
        for i in range(1, per):
            out = jnp.where(half == i, acc_ref[pair * per + i] / l_ref[pair * per + i], out)
        o_ref[:, pair * LANES:(pair + 1) * LANES] = out.astype(BF16)


def _dsa_prompt(q, qi, kw, ktb, vtb, kkt, batch, seq, tq):
    nq = seq // tq
    ktop = min(TOPK_MAX, seq // 4)
    qrow = lambda w: pl.BlockSpec((tq, w), lambda b, j: (b * nq + j, 0))
    full = lambda r: pl.BlockSpec((None, r, seq), lambda b, j: (b, 0, 0))
    return pl.pallas_call(
        functools.partial(_dsa_prompt_kernel, tq=tq, ktop=ktop),
        grid=(batch, nq),
        in_specs=[qrow(512), qrow(512), qrow(LANES), full(512), full(512), full(LANES)],
        out_specs=qrow(512),
        out_shape=jax.ShapeDtypeStruct((batch * seq, D_ATTN), BF16),
        scratch_shapes=[pltpu.VMEM((nq, tq, tq), F32), pltpu.VMEM((tq, LANES), F32), pltpu.VMEM((tq, LANES), F32),
                        pltpu.VMEM((tq, LANES), F32), pltpu.VMEM((N_HEADS_A, tq, LANES), F32),
                        pltpu.VMEM((N_HEADS_A, tq, LANES), F32), pltpu.VMEM((N_HEADS_A, tq, LANES), F32)],
        compiler_params=_params("parallel", "arbitrary"),
        name="dsa_prompt",
    )(q, qi, kw, ktb, vtb, kkt)


def _sample_scores_kernel(pt_ref, a_ref, w_ref, kin_ref, *rest, pages):
    page_refs = rest[:pages]
    sp_ref, sn_ref = rest[pages], rest[pages + 1]
    t_new = sn_ref.shape[0]
    a = a_ref[...]
    w = w_ref[...]

    def head_sum(s):
        r = jnp.maximum(s, 0.0) * w
        out = r[0:t_new]
        for h in range(1, N_HEADS_IDX):
            out = out + r[h * t_new:(h + 1) * t_new]
        return out

    for i in range(pages):
        ikt = page_refs[i][...].astype(BF16)
        sp_ref[:, i * PAGE_SIZE:(i + 1) * PAGE_SIZE] = head_sum(_dot(a, ikt))

    @pl.when(pl.program_id(1) == 0)
    def _():
        sn = head_sum(_dot_nt(a, kin_ref[...]))
        ti = lax.broadcasted_iota(jnp.int32, sn.shape, 0)
        si = lax.broadcasted_iota(jnp.int32, sn.shape, 1)
        sn_ref[...] = jnp.where(si <= ti, sn, -jnp.inf)


def _sample_scores(layer, page_table, a_rows, w_rows, ki_new_pad, cache_idx_kt, n_pool, pages):
    batch, n_pages = page_table.shape
    ht = a_rows.shape[1]
    t_new = ht // N_HEADS_IDX
    page_spec = lambda i: pl.BlockSpec(
        (None, D_IDX, PAGE_SIZE), lambda b, s, pt: (layer * n_pool + pt[b, s * pages + i], 0, 0))
    grid_spec = pltpu.PrefetchScalarGridSpec(
        num_scalar_prefetch=1,
        grid=(batch, n_pages // pages),
        in_specs=[pl.BlockSpec((None, ht, D_IDX), lambda b, s, pt: (b, 0, 0)),
                  pl.BlockSpec((None, ht, LANES), lambda b, s, pt: (b, 0, 0)),
                  pl.BlockSpec((None, LANES, D_IDX), lambda b, s, pt: (b, 0, 0))]
        + [page_spec(i) for i in range(pages)],
        out_specs=[pl.BlockSpec((None, t_new, pages * PAGE_SIZE), lambda b, s, pt: (b, 0, s)),
                   pl.BlockSpec((None, t_new, LANES), lambda b, s, pt: (b, 0, 0))],
    )
    return pl.pallas_call(
        functools.partial(_sample_scores_kernel, pages=pages),
        grid_spec=grid_spec,
        out_shape=[jax.ShapeDtypeStruct((batch, t_new, n_pages * PAGE_SIZE), F32),
                   jax.ShapeDtypeStruct((batch, t_new, LANES), F32)],
        compiler_params=_params("parallel", "arbitrary"),
        name="sample_scores",
    )(page_table, a_rows, w_rows, ki_new_pad, *([cache_idx_kt] * pages))


def _sample_select_kernel(sp_ref, sn_ref, selp_ref, seln_ref, sc_ref, lo_ref, hi_ref, flo_ref, *, width, ktop):
    rows, past = sp_ref.shape
    n_tiles = (past + LANES) // width
    for c in range(n_tiles):
        lo, hi = c * width, (c + 1) * width
        if hi <= past:
            sc_ref[c] = sp_ref[:, lo:hi]
        elif lo == past:
            sc_ref[c] = sn_ref[...]
        else:
            sc_ref[c] = jnp.concatenate([sp_ref[:, lo:past], sn_ref[...]], axis=1)

    def get_tile(c):
        return sc_ref[c]

    def set_tile(c, val):
        sc_ref[c] = val

    thr = _topk_threshold(get_tile, set_tile, n_tiles, rows, width, ktop, lo_ref, hi_ref, flo_ref)
    for c in range(n_tiles):
        lo, hi = c * width, (c + 1) * width
        sel = jnp.concatenate([jnp.where(ch >= thr, 1.0, 0.0) for ch in _chunks(sc_ref[c])], axis=1)
        if hi <= past:
            selp_ref[:, lo:hi] = sel
        elif lo == past:
            seln_ref[...] = sel
        else:
            selp_ref[:, lo:past] = sel[:, 0:past - lo]
            seln_ref[...] = sel[:, past - lo:]


def _sample_select(sp, sn, rows, width, ktop):
    n, past = sp.shape
    spec = lambda w: pl.BlockSpec((rows, w), lambda i: (i, 0))
    return pl.pallas_call(
        functools.partial(_sample_select_kernel, width=width, ktop=ktop),
        grid=(n // rows,),
        in_specs=[spec(past), spec(LANES)],
        out_specs=[spec(past), spec(LANES)],
        out_shape=[jax.ShapeDtypeStruct((n, past), F32), jax.ShapeDtypeStruct((n, LANES), F32)],
        scratch_shapes=[pltpu.VMEM(((past + LANES) // width, rows, width), F32), pltpu.VMEM((rows, LANES), F32),
                        pltpu.VMEM((rows, LANES), F32), pltpu.VMEM((rows, LANES), F32)],
        compiler_params=_params("parallel"),
        name="sample_select",
    )(sp, sn)


def _sample_attn_kernel(pt_ref, q_ref, selp_ref, seln_ref, kn_ref, vn_ref, *rest, pages):
    kt_refs = rest[:pages]
    vt_refs = rest[pages:2 * pages]
    o_ref, m_ref, l_ref, acc_ref = rest[2 * pages:]
    step = pl.program_id(1)
    t_new = o_ref.shape[0]
    q = q_ref[...]

    def tile_rows(sel):
        return jnp.concatenate([sel] * N_HEADS_A, axis=0)

    def update(s, pv):
        m_old = m_ref[...]
        m_new = jnp.maximum(m_old, jnp.max(s, axis=1, keepdims=True))
        alpha = jnp.exp2(m_old - m_new)
        p = jnp.exp2(s - m_new)
        l_ref[...] = alpha * l_ref[...] + jnp.sum(p, axis=1, keepdims=True)
        acc_ref[...] = alpha * acc_ref[...] + pv(p.astype(BF16))
        m_ref[...] = m_new

    @pl.when(step == 0)
    def _():
        m_ref[...] = jnp.full(m_ref.shape, NEG, F32)
        l_ref[...] = jnp.zeros(l_ref.shape, F32)
        acc_ref[...] = jnp.zeros(acc_ref.shape, F32)
        s = jnp.where(tile_rows(seln_ref[...]) > 0.5, _dot_nt(q, kn_ref[...]), NEG)
        update(s, lambda p: _dot(p, vn_ref[...]))

    s = jnp.concatenate([_dot(q, kt_refs[i][...].astype(BF16)) for i in range(pages)], axis=1)
    s = jnp.where(tile_rows(selp_ref[...]) > 0.5, s, NEG)

    def pv(p):
        out = _dot_nt(p[:, 0:PAGE_SIZE], vt_refs[0][...].astype(BF16))
        for i in range(1, pages):
            out = out + _dot_nt(p[:, i * PAGE_SIZE:(i + 1) * PAGE_SIZE], vt_refs[i][...].astype(BF16))
        return out

    update(s, pv)

    @pl.when(step == pl.num_programs(1) - 1)
    def _():
        res = acc_ref[...] / l_ref[...]
        lane_head = lax.broadcasted_iota(jnp.int32, (1, D_ATTN), 1) // HEAD_DIM_A
        out = jnp.zeros((t_new, D_ATTN), F32)
        for h in range(N_HEADS_A):
            out = out + jnp.where(lane_head == h, res[h * t_new:(h + 1) * t_new], 0.0)
        o_ref[...] = out.astype(BF16)


def _sample_attn(layer, page_table, q_rows, selp, seln, kn_pad, vn_pad, cache_kt, cache_vt, n_pool, pages):
    batch, n_pages = page_table.shape
    ht = q_rows.shape[1]
    t_new = ht // N_HEADS_A
    page_spec = lambda i: pl.BlockSpec(
        (None, D_ATTN, PAGE_SIZE), lambda b, s, pt: (layer * n_pool + pt[b, s * pages + i], 0, 0))
    per_b = lambda r, w: pl.BlockSpec((None, r, w), lambda b, s, pt: (b, 0, 0))
    grid_spec = pltpu.PrefetchScalarGridSpec(
        num_scalar_prefetch=1,
        grid=(batch, n_pages // pages),
        in_specs=[per_b(ht, D_ATTN),
                  pl.BlockSpec((None, t_new, pages * PAGE_SIZE), lambda b, s, pt: (b, 0, s)),
                  per_b(t_new, LANES), per_b(LANES, D_ATTN), per_b(LANES, D_ATTN)]
        + [page_spec(i) for i in range(pages)] * 2,
        out_specs=per_b(t_new, D_ATTN),
        scratch_shapes=[pltpu.VMEM((ht, 1), F32), pltpu.VMEM((ht, 1), F32), pltpu.VMEM((ht, D_ATTN), F32)],
    )
    return pl.pallas_call(
        functools.partial(_sample_attn_kernel, pages=pages),
        grid_spec=grid_spec,
        out_shape=jax.ShapeDtypeStruct((batch, t_new, D_ATTN), BF16),
        compiler_params=_params("parallel", "arbitrary"),
        name="sample_attn",
    )(page_table, q_rows, selp, seln, kn_pad, vn_pad, *([cache_kt] * pages), *([cache_vt] * pages))


def _merge_kernel(x_ref, a_ref, g_ref, ga_ref, gb_ref, wa_ref, wb_ref, wo_ref, n2_ref, x1_ref, h2_ref):
    mixed = (jax.nn.sigmoid(ga_ref[...]) * _dot(a_ref[...], wa_ref[...])
             + jax.nn.sigmoid(gb_ref[...]) * _dot(g_ref[...], wb_ref[...]))
    x1 = x_ref[...] + _dot(mixed.astype(BF16), wo_ref[...])
    x1_ref[...] = x1
    h2_ref[...] = (x1 * lax.rsqrt(jnp.mean(x1 * x1, axis=-1, keepdims=True) + EPS) * n2_ref[...]).astype(BF16)


def _merge(x2d, a_o, g_o, ga, gb, w_a, w_b, w_o, norm2, tm):
    n, d = x2d.shape
    row = lambda w: pl.BlockSpec((tm, w), lambda i: (i, 0))
    weights = (w_a.astype(BF16), w_b.astype(BF16), w_o.astype(BF16))
    return pl.pallas_call(
        _merge_kernel,
        grid=(n // tm,),
        in_specs=[row(d), row(D_ATTN), row(D_GLA_V), row(d), row(d)]
        + [_const_spec(w.shape) for w in weights] + [_const_spec((1, d))],
        out_specs=[row(d), row(d)],
        out_shape=[jax.ShapeDtypeStruct((n, d), F32), jax.ShapeDtypeStruct((n, d), BF16)],
        compiler_params=_params("parallel"),
        name="merge",
    )(x2d, a_o, g_o, ga, gb, *weights, norm2.reshape(1, d))


def _ffn_kernel(x1_ref, h2_ref, wg_ref, wu_ref, wd_ref, nf_ref, y_ref, *, n_split, final_norm):
    h2 = h2_ref[...]
    y = x1_ref[...]
    fc = wg_ref.shape[1] // n_split
    for i in range(n_split):
        g = _dot(h2, wg_ref[:, i * fc:(i + 1) * fc])
        u = _dot(h2, wu_ref[:, i * fc:(i + 1) * fc])
        act = (g * jax.nn.sigmoid(g) * u).astype(BF16)
        y = y + _dot(act, wd_ref[i * fc:(i + 1) * fc, :])
    if final_norm:
        y = y * lax.rsqrt(jnp.mean(y * y, axis=-1, keepdims=True) + EPS) * nf_ref[...]
    y_ref[...] = y


def _ffn(x1, h2, w_g, w_u, w_d, norm_final, tm, final_norm):
    n, d = x1.shape
    d_ff = w_g.shape[1]
    n_split = 2 if d_ff % (2 * LANES) == 0 else 1
    row = lambda: pl.BlockSpec((tm, d), lambda i: (i, 0))
    weights = (w_g.astype(BF16), w_u.astype(BF16), w_d.astype(BF16))
    return pl.pallas_call(
        functools.partial(_ffn_kernel, n_split=n_split, final_norm=final_norm),
        grid=(n // tm,),
        in_specs=[row(), row()] + [_const_spec(w.shape) for w in weights] + [_const_spec((1, d))],
        out_specs=row(),
        out_shape=jax.ShapeDtypeStruct((n, d), F32),
        compiler_params=_params("parallel"),
        name="ffn",
    )(x1, h2, *weights, norm_final.reshape(1, d))


def _pick(n, prefs):
    for p in prefs:
        if n % p == 0:
            return p
    return n


def kernel(x_prompt, x_sample, cache_k, cache_v, cache_idx_k, state_gla, page_table, norm1, w_in, w_gate_up,
           b_gate, g_gla_norm, w_branch_a, w_branch_b, w_out, norm2, w_ffn_gate, w_ffn_up, w_ffn_down, norm_final):
    depth = w_in.shape[0]
    bp, seq, d = x_prompt.shape
    bs, t_new, _ = x_sample.shape
    n_pool = cache_k.shape[1]
    n_pages = page_table.shape[1]
    past = n_pages * PAGE_SIZE
    xp = x_prompt.reshape(bp * seq, d)
    xs = x_sample.reshape(bs * t_new, d)
    cache_kt = cache_k.transpose(0, 1, 3, 4, 2).reshape(depth * n_pool, D_ATTN, PAGE_SIZE)
    cache_vt = cache_v.transpose(0, 1, 3, 4, 2).reshape(depth * n_pool, D_ATTN, PAGE_SIZE)
    cache_ikt = cache_idx_k.transpose(0, 1, 3, 2).reshape(depth * n_pool, D_IDX, PAGE_SIZE)
    outs = [[] for _ in range(8)]
    tq = _pick(seq, (256, 128))
    for l in range(depth):
        last = l == depth - 1
        (q, qi, kw, qkg, vg, la, r, ga, gb, kt, vt, ktb, vtb, kit, kkt) = _inproj(
            xp, norm1[l], w_in[l], w_gate_up[l], b_gate[l], bp, _pick(seq, (256, 128)), True)
        a_o = _dsa_prompt(q, qi, kw, ktb, vtb, kkt, bp, seq, tq)
        chunk = GLA_CHUNK if seq % GLA_CHUNK == 0 else seq
        g_o, st = _gla(qkg, vg, la, r, g_gla_norm[l], jnp.zeros((bp, DV_G, D_GLA_K), F32), bp, seq, chunk,
                       _pick(seq, (512, 256, 128, 64)))
        tm = _pick(bp * seq, (512, 256, 128))
        x1, h2 = _merge(xp, a_o, g_o, ga, gb, w_branch_a[l], w_branch_b[l], w_out[l], norm2[l], tm)
        xp = _ffn(x1, h2, w_ffn_gate[l], w_ffn_up[l], w_ffn_down[l], norm_final, _pick(bp * seq, (256, 128)), last)
        outs[0].append(kt.reshape(bp, N_HEADS_A, HEAD_DIM_A, seq).transpose(0, 3, 1, 2))
        outs[1].append(vt.reshape(bp, N_HEADS_A, HEAD_DIM_A, seq).transpose(0, 3, 1, 2))
        outs[2].append(kit.transpose(0, 2, 1))
        outs[3].append(_state_from_t(st))

        (q, qi, kw, qkg, vg, la, r, ga, gb, k, v, kb, vb, kk) = _inproj(
            xs, norm1[l], w_in[l], w_gate_up[l], b_gate[l], 1, _pick(bs * t_new, (256, 128)), False)
        a_rows = qi.reshape(bs, t_new, N_HEADS_IDX, D_IDX).transpose(0, 2, 1, 3).reshape(bs, N_HEADS_IDX * t_new, D_IDX)
        w_rows = (kw[:, D_IDX:D_IDX + N_HEADS_IDX] * (N_HEADS_IDX ** -0.5 * D_IDX ** -0.5)).reshape(
            bs, t_new, N_HEADS_IDX).transpose(0, 2, 1).reshape(bs, N_HEADS_IDX * t_new, 1)
        w_rows = jnp.broadcast_to(w_rows, (bs, N_HEADS_IDX * t_new, LANES))
        pad_rows = lambda a: jnp.pad(a.reshape(bs, t_new, a.shape[-1]), ((0, 0), (0, LANES - t_new), (0, 0)))
        sp, sn = _sample_scores(l, page_table, a_rows, w_rows, pad_rows(kk[:, 0:D_IDX]), cache_ikt, n_pool,
                                _pick(n_pages, (16, 8, 4, 2, 1)))
        ktop = min(TOPK_MAX, (past + t_new) // 4)
        width = 5 * LANES if (past + LANES) % (5 * LANES) == 0 else LANES
        rows = _pick(bs * t_new, (128, 64, 32, 16, 8))
        selp, seln = _sample_select(sp.reshape(bs * t_new, past), sn.reshape(bs * t_new, LANES), rows, width, ktop)
        head_of_lane = jnp.arange(D_ATTN) // HEAD_DIM_A
        q_rows = jnp.where(head_of_lane[None, None, None, :] == jnp.arange(N_HEADS_A)[None, :, None, None],
                           q.reshape(bs, 1, t_new, D_ATTN), jnp.zeros((), BF16)).reshape(bs, N_HEADS_A * t_new, D_ATTN)
        a_o = _sample_attn(l, page_table, q_rows, selp.reshape(bs, t_new, past), seln.reshape(bs, t_new, LANES),
                           pad_rows(kb), pad_rows(vb), cache_kt, cache_vt, n_pool, _pick(n_pages, (8, 4, 2, 1)))
        gchunk = 16
        pad_g = lambda a: jnp.pad(a.reshape(bs, t_new, a.shape[-1]), ((0, 0), (0, gchunk - t_new), (0, 0))).reshape(
            bs * gchunk, a.shape[-1])
        g_o, st = _gla(pad_g(qkg), pad_g(vg), pad_g(la), pad_g(r), g_gla_norm[l], _state_to_t(state_gla[l].astype(F32)),
                       bs, gchunk, gchunk, gchunk)
        g_o = g_o.reshape(bs, gchunk, D_GLA_V)[:, 0:t_new].reshape(bs * t_new, D_GLA_V)
        tm = _pick(bs * t_new, (512, 256, 128))
        x1, h2 = _merge(xs, a_o.reshape(bs * t_new, D_ATTN), g_o, ga, gb, w_branch_a[l], w_branch_b[l], w_out[l],
                        norm2[l], tm)
        xs = _ffn(x1, h2, w_ffn_gate[l], w_ffn_up[l], w_ffn_down[l], norm_final, _pick(bs * t_new, (256, 128)), last)
        outs[4].append(k.reshape(bs, t_new, N_HEADS_A, HEAD_DIM_A))
        outs[5].append(v.reshape(bs, t_new, N_HEADS_A, HEAD_DIM_A))
        outs[6].append(kw[:, 0:D_IDX].reshape(bs, t_new, D_IDX))
        outs[7].append(_state_from_t(st))
    return (xp.reshape(bp, seq, d), xs.reshape(bs, t_new, d)) + tuple(jnp.stack(o) for o in outs)
```

```python
import functools

import jax
import jax.numpy as jnp
from jax import lax
from jax.experimental import pallas as pl
from jax.experimental.pallas import tpu as pltpu

F32 = jnp.float32
BF16 = jnp.bfloat16

D_ATTN = 512
N_HEADS_A = 8
HEAD_DIM_A = 64
N_HEADS_IDX = 8
D_IDX = 64
TOPK_MAX = 256
N_HEADS_G = 4
DK_G = 64
DV_G = 128
D_GLA_K = N_HEADS_G * DK_G
D_GLA_V = N_HEADS_G * DV_G
GATE_RANK = 16
GATE_TAU = 16.0
GLA_CHUNK = 64
EPS = 1e-6
PAGE_SIZE = 128

LANES = 128
NEG = -1e30
LOWEST = -3.0e38
MAX_BISECT = 64
Q_SCALE = HEAD_DIM_A ** -0.5 * 1.4426950408889634
VMEM_LIMIT = 56 * 1024 * 1024


def _dot(a, b):
    return jnp.dot(a, b, preferred_element_type=F32)


def _dot_nt(a, b):
    return lax.dot_general(a, b, (((1,), (1,)), ((), ())), preferred_element_type=F32)


def _dot_tn(a, b):
    return lax.dot_general(a, b, (((0,), (0,)), ((), ())), preferred_element_type=F32)


def _params(*sem):
    return pltpu.CompilerParams(dimension_semantics=sem, vmem_limit_bytes=VMEM_LIMIT)


def _const_spec(shape):
    n = len(shape)
    return pl.BlockSpec(shape, lambda *_: (0,) * n)


def _chunks(x):
    return [x[:, i * LANES:(i + 1) * LANES] for i in range(x.shape[1] // LANES)]


def _rep(col, rows):
    return jnp.broadcast_to(col, (rows, LANES))


def _inproj_kernel(x_ref, g_ref, wc_ref, wd_ref, we_ref, wgu_ref, bg_ref, *refs, key_major):
    x = x_ref[...]
    h = (x * lax.rsqrt(jnp.mean(x * x, axis=-1, keepdims=True) + EPS) * g_ref[...]).astype(BF16)
    if key_major:
        (wqt_ref, wkvt_ref, wk_ref, wkk_ref, wkit_ref, wwit_ref,
         qkg_ref, vg_ref, la_ref, r_ref, ga_ref, gb_ref,
         qt_ref, qit_ref, wt_ref, kb_ref, kk_ref, kt_ref, vt_ref, vtb_ref, kit_ref) = refs
        qq = _dot_nt(wqt_ref[...], h)
        qt_ref[...] = (qq[0:512] * Q_SCALE).astype(BF16)
        qit_ref[...] = qq[512:1024].astype(BF16)
        kv = _dot_nt(wkvt_ref[...], h)
        kt_ref[...] = kv[0:512]
        vt_ref[...] = kv[512:1024]
        vtb_ref[...] = kv[512:1024].astype(BF16)
        kb_ref[...] = _dot(h, wk_ref[...]).astype(BF16)
        kk_ref[...] = _dot(h, wkk_ref[...]).astype(BF16)
        kit_ref[...] = _dot_nt(wkit_ref[...], h)
        wt_ref[...] = _dot_nt(wwit_ref[...], h)[0:N_HEADS_IDX] * (N_HEADS_IDX ** -0.5 * D_IDX ** -0.5)
    else:
        (wq_ref, wkv_ref, wkk_ref, wkw_ref,
         qkg_ref, vg_ref, la_ref, r_ref, ga_ref, gb_ref,
         q_ref, qi_ref, kw_ref, k_ref, v_ref, kb_ref, vb_ref, kk_ref) = refs
        a = _dot(h, wq_ref[...])
        q_ref[...] = (a[:, 0:512] * Q_SCALE).astype(BF16)
        qi_ref[...] = a[:, 512:1024].astype(BF16)
        kw_ref[...] = _dot(h, wkw_ref[...])
        kv = _dot(h, wkv_ref[...])
        k_ref[...] = kv[:, 0:512]
        v_ref[...] = kv[:, 512:1024]
        kb_ref[...] = kv[:, 0:512].astype(BF16)
        vb_ref[...] = kv[:, 512:1024].astype(BF16)
        kk_ref[...] = _dot(h, wkk_ref[...]).astype(BF16)
    c = _dot(h, wc_ref[...])
    qkg_ref[...] = c[:, 0:512]
    vg_ref[...] = c[:, 512:1024]
    glr = _dot(h, wd_ref[...]).astype(BF16)
    z = _dot(glr, wgu_ref[...]) + bg_ref[...]
    la_ref[...] = (jnp.minimum(z, 0.0) - jnp.log1p(jnp.exp(-jnp.abs(z)))) * (1.0 / GATE_TAU)
    e = _dot(h, we_ref[...])
    r_ref[...] = e[:, 0:512]
    ga_ref[...] = e[:, 512:1536]
    gb_ref[...] = e[:, 1536:2560]


def _inproj(x2d, norm_g, w_in, w_gate_up, b_gate, batch, tm, key_major):
    n, d = x2d.shape
    t_len = n // batch
    nt = t_len // tm
    wb = w_in.astype(BF16)
    wq = jnp.concatenate([wb[:, 0:512], wb[:, 1536:2048]], axis=1)
    wkv = wb[:, 512:1536]
    wki = wb[:, 2048:2112]
    wwi = wb[:, 2112:2120]
    wkk = jnp.concatenate([wki, wki], axis=1)
    wc = wb[:, 2120:3144]
    wd = jnp.concatenate([wb[:, 3144:3160], jnp.zeros((d, LANES - GATE_RANK), BF16)], axis=1)
    we = wb[:, 3160:5720]
    wgu = jnp.concatenate([w_gate_up.astype(BF16), jnp.zeros((LANES - GATE_RANK, D_GLA_K), BF16)], axis=0)
    row = lambda w: pl.BlockSpec((tm, w), lambda b, i: (b * nt + i, 0))
    col = lambda r: pl.BlockSpec((None, r, tm), lambda b, i: (b, 0, i))
    rows_out = lambda outs: ([row(w) for w, _ in outs], [jax.ShapeDtypeStruct((n, w), dt) for w, dt in outs])
    cols_out = lambda outs: ([col(r) for r, _ in outs],
                             [jax.ShapeDtypeStruct((batch, r, t_len), dt) for r, dt in outs])
    out_specs, out_shape = rows_out([(512, F32), (512, F32), (256, F32), (512, F32), (1024, F32), (1024, F32)])
    if key_major:
        wwit = jnp.concatenate([wwi.T, jnp.zeros((16 - N_HEADS_IDX, d), BF16)], axis=0)
        flavor_w = (wq.T, wkv.T, wb[:, 512:1024], wkk, wki.T, wwit)
        s1, h1 = cols_out([(512, BF16), (512, BF16), (N_HEADS_IDX, F32)])
        s2, h2 = rows_out([(512, BF16), (LANES, BF16)])
        s3, h3 = cols_out([(512, F32), (512, F32), (512, BF16), (D_IDX, F32)])
        out_specs, out_shape = out_specs + s1 + s2 + s3, out_shape + h1 + h2 + h3
    else:
        wkw = jnp.concatenate([wki, wwi, jnp.zeros((d, LANES - 72), BF16)], axis=1)
        flavor_w = (wq, wkv, wkk, wkw)
        s1, h1 = rows_out([(512, BF16), (512, BF16), (LANES, F32), (512, F32), (512, F32), (512, BF16),
                           (512, BF16), (LANES, BF16)])
        out_specs, out_shape = out_specs + s1, out_shape + h1
    weights = (wc, wd, we, wgu)
    return pl.pallas_call(
        functools.partial(_inproj_kernel, key_major=key_major),
        grid=(batch, nt),
        in_specs=[row(d), _const_spec((1, d))] + [_const_spec(w.shape) for w in weights]
        + [_const_spec((1, D_GLA_K))] + [_const_spec(w.shape) for w in flavor_w],
        out_specs=out_specs,
        out_shape=out_shape,
        compiler_params=_params("parallel", "parallel"),
        name="inproj",
    )(x2d, norm_g.reshape(1, d), *weights, b_gate.reshape(1, D_GLA_K), *flavor_w)


def _gla_kernel(qk_ref, v_ref, la_ref, r_ref, gn_ref, s0_ref, go_ref, sfin_ref, st_ref, *, chunk, n_chunks):
    step = pl.program_id(1)

    @pl.when(step == 0)
    def _():
        st_ref[...] = s0_ref[...]

    ri = lax.broadcasted_iota(jnp.int32, (chunk, chunk), 0)
    ci = lax.broadcasted_iota(jnp.int32, (chunk, chunk), 1)
    tril = ci <= ri
    tri = jnp.where(tril, 1.0, 0.0).astype(F32)
    lane_head = lax.broadcasted_iota(jnp.int32, (1, D_GLA_K), 1) // DK_G
    mid = chunk // 2 - 1
    gn = gn_ref[...]

    def body(c, carry):
        for g in range(st_ref.shape[0]):
            one_chunk(c, g)
        return carry

    def one_chunk(c, g):
        r0 = pl.multiple_of(c * chunk, chunk)
        qk = qk_ref[g, pl.ds(r0, chunk), :]
        q = qk[:, 0:D_GLA_K] * DK_G ** -0.5
        k = qk[:, D_GLA_K:2 * D_GLA_K]
        la = la_ref[g, pl.ds(r0, chunk), :]
        b = jnp.dot(tri, la, precision=lax.Precision.HIGHEST, preferred_element_type=F32)
        b_mid = b[mid:mid + 1, :]
        b_end = b[chunk - 1:chunk, :]
        qe = q * jnp.exp(b)
        qt = q * jnp.exp(b - b_mid)
        ktb = (k * jnp.exp(b_mid - b)).astype(BF16)
        kdb = (k * jnp.exp(b_end - b)).astype(BF16)
        st = st_ref[g]
        stb = st.astype(BF16)
        new_st = st * jnp.exp(b_end)
        for h in range(N_HEADS_G):
            hm = lane_head == h
            qt_h = jnp.where(hm, qt, 0.0).astype(BF16)
            qe_h = jnp.where(hm, qe, 0.0).astype(BF16)
            att = jnp.where(tril, _dot_nt(qt_h, ktb), 0.0)
            vb = v_ref[g, pl.ds(r0, chunk), h * DV_G:(h + 1) * DV_G].astype(BF16)
            o = _dot_nt(qe_h, stb) + _dot(att.astype(BF16), vb)
            new_st = new_st + jnp.where(hm, _dot_tn(vb, kdb), 0.0)
            on = o * lax.rsqrt(jnp.mean(o * o, axis=-1, keepdims=True) + EPS) * gn
            rr = r_ref[g, pl.ds(r0, chunk), h * DV_G:(h + 1) * DV_G]
            go_ref[g, pl.ds(r0, chunk), h * DV_G:(h + 1) * DV_G] = (on * (rr * jax.nn.sigmoid(rr))).astype(BF16)
        st_ref[g] = new_st

    lax.fori_loop(0, n_chunks, body, 0)

    @pl.when(step == pl.num_programs(1) - 1)
    def _():
        sfin_ref[...] = st_ref[...]


def _gla(qkg, vg, la, r, g_norm, s0_t, batch, t_len, chunk, rows_per_step, group):
    n_steps = t_len // rows_per_step
    row = lambda w: pl.BlockSpec((group, rows_per_step, w), lambda b, s: (b, s, 0))
    st_spec = pl.BlockSpec((group, DV_G, D_GLA_K), lambda b, s: (b, 0, 0))
    seqs = lambda a: a.reshape(batch, t_len, a.shape[-1])
    g_o, st = pl.pallas_call(
        functools.partial(_gla_kernel, chunk=chunk, n_chunks=rows_per_step // chunk),
        grid=(batch // group, n_steps),
        in_specs=[row(512), row(512), row(256), row(512), _const_spec((1, DV_G)), st_spec],
        out_specs=[row(512), st_spec],
        out_shape=[jax.ShapeDtypeStruct((batch, t_len, D_GLA_V), BF16),
                   jax.ShapeDtypeStruct((batch, DV_G, D_GLA_K), F32)],
        scratch_shapes=[pltpu.VMEM((group, DV_G, D_GLA_K), F32)],
        compiler_params=_params("parallel", "arbitrary"),
        name="gla",
    )(seqs(qkg), seqs(vg), seqs(la), seqs(r), g_norm.reshape(1, DV_G), s0_t)
    return g_o.reshape(batch * t_len, D_GLA_V), st


def _state_to_t(s):
    b = s.shape[0]
    return s.transpose(0, 3, 1, 2).reshape(b, DV_G, D_GLA_K)


def _state_from_t(st):
    b = st.shape[0]
    return st.reshape(b, DV_G, N_HEADS_G, DK_G).transpose(0, 2, 3, 1)


def _topk_threshold(get_tile, set_tile, n_tiles, rows, width, ktop, lo_ref, hi_ref, flo_ref):
    kf = float(ktop)
    one = lambda m: jnp.where(m, 1.0, 0.0)
    zeros = jnp.zeros((rows, LANES), F32)
    rsum = lambda a: _rep(jnp.sum(a, axis=1, keepdims=True), rows)

    def acc_chunks(blk, acc, fn, op):
        for ch in _chunks(blk):
            acc = op(acc, fn(ch))
        return acc

    def stats(c, st):
        mx, mn, nv, c0, c0e = st
        blk = get_tile(c)
        mx = acc_chunks(blk, mx, lambda ch: ch, jnp.maximum)
        mn = acc_chunks(blk, mn, lambda ch: jnp.where(ch > -jnp.inf, ch, jnp.inf), jnp.minimum)
        nv = acc_chunks(blk, nv, lambda ch: one(ch > -jnp.inf), jnp.add)
        c0 = acc_chunks(blk, c0, lambda ch: one(ch > 0.0), jnp.add)
        c0e = acc_chunks(blk, c0e, lambda ch: one(ch >= 0.0), jnp.add)
        return mx, mn, nv, c0, c0e

    mx, mn, nv, c0, c0e = lax.fori_loop(
        0, n_tiles, stats,
        (jnp.full((rows, LANES), -jnp.inf, F32), jnp.full((rows, LANES), jnp.inf, F32), zeros, zeros, zeros))
    rowmax = _rep(jnp.max(mx, axis=1, keepdims=True), rows)
    rowmin = _rep(jnp.min(mn, axis=1, keepdims=True), rows)
    nvalid, cnt0, cnt0e = rsum(nv), rsum(c0), rsum(c0e)

    def count_gt(thr):
        def body(c, acc):
            return acc_chunks(get_tile(c), acc, lambda ch: one(ch > thr), jnp.add)
        return rsum(lax.fori_loop(0, n_tiles, body, zeros))

    small = nvalid <= kf
    ztie = jnp.logical_and(jnp.logical_and(cnt0 < kf, cnt0e >= kf), jnp.logical_not(small))
    frozen = jnp.logical_or(small, ztie)
    pos = cnt0 >= kf
    lo_ref[...] = jnp.where(frozen, 0.0, jnp.where(pos, 0.0, jnp.minimum(rowmin - 1.0, rowmin * 2.0)))
    hi_ref[...] = jnp.where(frozen, 0.0, jnp.where(pos, rowmax, 0.0))
    flo_ref[...] = jnp.where(frozen, kf, jnp.where(pos, cnt0, nvalid))

    def cond(st):
        it, pending = st
        return jnp.logical_and(it < MAX_BISECT, pending > 0)

    def bisect(st):
        it, _ = st
        lo, hi = lo_ref[...], hi_ref[...]
        mid = 0.5 * (lo + hi)
        c = count_gt(mid)
        ge = c >= kf
        flo = jnp.where(ge, c, flo_ref[...])
        lo_ref[...] = jnp.where(ge, mid, lo)
        hi_ref[...] = jnp.where(ge, hi, mid)
        flo_ref[...] = flo
        return it + 1, jnp.max(one(flo != kf)).astype(jnp.int32)

    pending0 = jnp.max(one(flo_ref[...] != kf)).astype(jnp.int32)
    lax.while_loop(cond, bisect, (jnp.int32(0), pending0))

    lo = lo_ref[...]

    def tmin(c, acc):
        return acc_chunks(get_tile(c), acc, lambda ch: jnp.where(ch > lo, ch, jnp.inf), jnp.minimum)

    thr = _rep(jnp.min(lax.fori_loop(0, n_tiles, tmin, jnp.full((rows, LANES), jnp.inf, F32)),
                       axis=1, keepdims=True), rows)
    thr = jnp.where(ztie, 0.0, thr)
    thr = jnp.where(small, LOWEST, thr)

    def gteq(c, st):
        g, e = st
        blk = get_tile(c)
        return (acc_chunks(blk, g, lambda ch: one(ch > thr), jnp.add),
                acc_chunks(blk, e, lambda ch: one(ch == thr), jnp.add))

    g, e = lax.fori_loop(0, n_tiles, gteq, (zeros, zeros))
    need = kf - rsum(g)
    any_tie = jnp.max(one(rsum(e) > need)) > 0.0

    @pl.when(any_tie)
    def _():
        ri = lax.broadcasted_iota(jnp.int32, (width, width), 0)
        ci = lax.broadcasted_iota(jnp.int32, (width, width), 1)
        upper = jnp.where(ri <= ci, 1.0, 0.0).astype(BF16)
        thr_c, need_c = thr[:, 0:1], need[:, 0:1]

        def drop(c, seen):
            blk = get_tile(c)
            eq = blk == thr_c
            eqf = one(eq)
            rank = _dot(eqf.astype(BF16), upper) + seen - 1.0
            set_tile(c, jnp.where(jnp.logical_and(eq, rank >= need_c), -jnp.inf, blk))
            return seen + jnp.sum(eqf, axis=1, keepdims=True)

        lax.fori_loop(0, n_tiles, drop, jnp.zeros((rows, 1), F32))

    return thr


SUB = 8


def _groups(x):
    return [x[i * SUB:(i + 1) * SUB] for i in range(x.shape[0] // SUB)]


def _tree(op, xs):
    xs = list(xs)
    while len(xs) > 1:
        xs = [op(xs[i], xs[i + 1]) if i + 1 < len(xs) else xs[i] for i in range(0, len(xs), 2)]
    return xs[0]


def _col(op, x8):
    return jnp.broadcast_to(op(x8, axis=0, keepdims=True), x8.shape)


def _topk_threshold_cols(get_tile, set_tile, n_tiles, tk, n, ktop, lo_ref, hi_ref, flo_ref):
    kf = float(ktop)
    one = lambda m: jnp.where(m, 1.0, 0.0)
    zeros = jnp.zeros((SUB, n), F32)
    total = lambda a: _col(jnp.sum, a)

    def stats(c, st):
        mx, mn, nv, c0, c0e = st
        g = _groups(get_tile(c))
        mx = jnp.maximum(mx, _tree(jnp.maximum, g))
        mn = jnp.minimum(mn, _tree(jnp.minimum, [jnp.where(x > -jnp.inf, x, jnp.inf) for x in g]))
        nv = nv + _tree(jnp.add, [one(x > -jnp.inf) for x in g])
        c0 = c0 + _tree(jnp.add, [one(x > 0.0) for x in g])
        c0e = c0e + _tree(jnp.add, [one(x >= 0.0) for x in g])
        return mx, mn, nv, c0, c0e

    mx, mn, nv, c0, c0e = lax.fori_loop(
        0, n_tiles, stats,
        (jnp.full((SUB, n), -jnp.inf, F32), jnp.full((SUB, n), jnp.inf, F32), zeros, zeros, zeros))
    colmax, colmin = _col(jnp.max, mx), _col(jnp.min, mn)
    nvalid, cnt0, cnt0e = total(nv), total(c0), total(c0e)

    def count_gt(thr):
        def body(c, acc):
            return acc + _tree(jnp.add, [one(x > thr) for x in _groups(get_tile(c))])
        return total(lax.fori_loop(0, n_tiles, body, zeros))

    small = nvalid <= kf
    ztie = jnp.logical_and(jnp.logical_and(cnt0 < kf, cnt0e >= kf), jnp.logical_not(small))
    frozen = jnp.logical_or(small, ztie)
    pos = cnt0 >= kf
    lo_ref[...] = jnp.where(frozen, 0.0, jnp.where(pos, 0.0, jnp.minimum(colmin - 1.0, colmin * 2.0)))
    hi_ref[...] = jnp.where(frozen, 0.0, jnp.where(pos, colmax, 0.0))
    flo_ref[...] = jnp.where(frozen, kf, jnp.where(pos, cnt0, nvalid))

    def cond(st):
        it, pending = st
        return jnp.logical_and(it < MAX_BISECT, pending > 0)

    def bisect(st):
        it, _ = st
        lo, hi = lo_ref[...], hi_ref[...]
        mid = 0.5 * (lo + hi)
        c = count_gt(mid)
        ge = c >= kf
        flo = jnp.where(ge, c, flo_ref[...])
        lo_ref[...] = jnp.where(ge, mid, lo)
        hi_ref[...] = jnp.where(ge, hi, mid)
        flo_ref[...] = flo
        return it + 1, jnp.max(one(flo != kf)).astype(jnp.int32)

    pending0 = jnp.max(one(flo_ref[...] != kf)).astype(jnp.int32)
    lax.while_loop(cond, bisect, (jnp.int32(0), pending0))

    lo = lo_ref[...]

    def tmin(c, acc):
        return jnp.minimum(acc, _tree(jnp.minimum, [jnp.where(x > lo, x, jnp.inf) for x in _groups(get_tile(c))]))

    thr = _col(jnp.min, lax.fori_loop(0, n_tiles, tmin, jnp.full((SUB, n), jnp.inf, F32)))
    thr = jnp.where(ztie, 0.0, thr)
    thr = jnp.where(small, LOWEST, thr)

    def gteq(c, st):
        g, e = st
        grp = _groups(get_tile(c))
        return (g + _tree(jnp.add, [one(x > thr) for x in grp]), e + _tree(jnp.add, [one(x == thr) for x in grp]))

    g, e = lax.fori_loop(0, n_tiles, gteq, (zeros, zeros))
    need = kf - total(g)
    any_tie = jnp.max(one(total(e) > need)) > 0.0

    @pl.when(any_tie)
    def _():
        ri = lax.broadcasted_iota(jnp.int32, (tk, tk), 0)
        ci = lax.broadcasted_iota(jnp.int32, (tk, tk), 1)
        lower = jnp.where(ci <= ri, 1.0, 0.0).astype(BF16)
        thr_r, need_r = thr[0:1], need[0:1]

        def drop(c, seen):
            blk = get_tile(c)
            eq = blk == thr_r
            eqf = one(eq)
            rank = _dot(lower, eqf.astype(BF16)) + seen - 1.0
            set_tile(c, jnp.where(jnp.logical_and(eq, rank >= need_r), -jnp.inf, blk))
            return seen + jnp.sum(eqf, axis=0, keepdims=True)

        lax.fori_loop(0, n_tiles, drop, jnp.zeros((1, n), F32))

    return thr


def _dsa_prompt_kernel(qt_ref, qit_ref, wt_ref, k_ref, vt_ref, kk_ref, o_ref,
                       sc_ref, lo_ref, hi_ref, flo_ref, m_ref, l_ref, acc_ref, sa_ref, sb_ref, *, tq, ktop):
    j = pl.program_id(1)
    n_tiles = j + 1
    t0 = j * tq
    qpos = t0 + lax.broadcasted_iota(jnp.int32, (1, tq), 1)
    kiota = lax.broadcasted_iota(jnp.int32, (tq, 1), 0)
    per = LANES // HEAD_DIM_A
    reps = tq // SUB
    zero_rows = jnp.zeros((LANES - D_IDX, tq), BF16)

    def tall(x8):
        return pltpu.repeat(x8, reps, axis=0)

    qit = qit_ref[...]
    wt = wt_ref[...]
    qih = [jnp.concatenate([qit[h * D_IDX:(h + 1) * D_IDX], zero_rows], axis=0) for h in range(N_HEADS_IDX)]
    wrow = [tall(jnp.broadcast_to(wt[h:h + 1], (SUB, tq))) for h in range(N_HEADS_IDX)]

    def score_tile(c, carry):
        k0 = pl.multiple_of(c * tq, tq)
        kk = kk_ref[pl.ds(k0, tq), :]
        acc = jnp.zeros((tq, tq), F32)
        for h in range(N_HEADS_IDX):
            acc = acc + jnp.maximum(_dot(kk, qih[h]), 0.0) * wrow[h]
        sc_ref[c] = jnp.where(k0 + kiota <= qpos, acc, -jnp.inf)
        return carry

    lax.fori_loop(0, n_tiles, score_tile, 0)

    def get_tile(c):
        return sc_ref[c]

    def set_tile(c, val):
        sc_ref[c] = val

    thr = tall(_topk_threshold_cols(get_tile, set_tile, n_tiles, tq, tq, ktop, lo_ref, hi_ref, flo_ref))

    qt = qt_ref[...]
    zq = jnp.zeros((HEAD_DIM_A, tq), BF16)
    qh = []
    for h in range(N_HEADS_A):
        rows = qt[h * HEAD_DIM_A:(h + 1) * HEAD_DIM_A]
        qh.append(jnp.concatenate([zq] * (h % per) + [rows] + [zq] * (per - 1 - h % per), axis=0))
    m_ref[...] = jnp.full(m_ref.shape, NEG, F32)
    l_ref[...] = jnp.zeros(l_ref.shape, F32)
    acc_ref[...] = jnp.zeros(acc_ref.shape, F32)

    ta = tq // 2
    thr_a = thr[0:ta]
    last = 2 * n_tiles - 1

    def logits(c, buf):
        k0 = pl.multiple_of(c * ta, ta)
        for h in range(N_HEADS_A):
            buf[h] = _dot(k_ref[pl.ds(k0, ta), (h // per) * LANES:(h // per + 1) * LANES], qh[h])

    def softmax_pv(c, half, buf):
        k0 = pl.multiple_of(c * ta, ta)
        bias = jnp.where(sc_ref[c // 2, half * ta:(half + 1) * ta, :] >= thr_a, 0.0, NEG)
        for h in range(N_HEADS_A):
            s = buf[h] + bias
            vh = vt_ref[h * HEAD_DIM_A:(h + 1) * HEAD_DIM_A, pl.ds(k0, ta)]
            m_old = m_ref[h]
            m_new = jnp.maximum(m_old, _col(jnp.max, _tree(jnp.maximum, _groups(s))))
            alpha = jnp.exp2(m_old - m_new)
            p = jnp.exp2(s - pltpu.repeat(m_new, ta // SUB, axis=0))
            l_ref[h] = alpha * l_ref[h] + _col(jnp.sum, _tree(jnp.add, _groups(p)))
            acc_ref[h] = pltpu.repeat(alpha, HEAD_DIM_A // SUB, axis=0) * acc_ref[h] + _dot(vh, p.astype(BF16))
            m_ref[h] = m_new

    logits(0, sa_ref)

    def att_tile(i, carry):
        logits(2 * i + 1, sb_ref)
        softmax_pv(2 * i, 0, sa_ref)
        logits(jnp.minimum(2 * i + 2, last), sa_ref)
        softmax_pv(2 * i + 1, 1, sb_ref)
        return carry

    lax.fori_loop(0, n_tiles, att_tile, 0)

    for h in range(N_HEADS_A):
        o_ref[h * HEAD_DIM_A:(h + 1) * HEAD_DIM_A, :] = (
            acc_ref[h] / pltpu.repeat(l_ref[h], HEAD_DIM_A // SUB, axis=0)).astype(BF16)


def _dsa_prompt(qt, qit, wt, kb, vtb, kk, batch, seq, tq):
    nq = seq // tq
    ktop = min(TOPK_MAX, seq // 4)
    qcol = lambda r: pl.BlockSpec((None, r, tq), lambda b, j: (b, 0, j))
    rows = lambda w: pl.BlockSpec((seq, w), lambda b, j: (b, 0))
    return pl.pallas_call(
        functools.partial(_dsa_prompt_kernel, tq=tq, ktop=ktop),
        grid=(batch, nq),
        in_specs=[qcol(512), qcol(512), qcol(N_HEADS_IDX), rows(512),
                  pl.BlockSpec((None, 512, seq), lambda b, j: (b, 0, 0)), rows(LANES)],
        out_specs=qcol(512),
        out_shape=jax.ShapeDtypeStruct((batch, D_ATTN, seq), BF16),
        scratch_shapes=[pltpu.VMEM((nq, tq, tq), F32), pltpu.VMEM((SUB, tq), F32), pltpu.VMEM((SUB, tq), F32),
                        pltpu.VMEM((SUB, tq), F32), pltpu.VMEM((N_HEADS_A, SUB, tq), F32),
                        pltpu.VMEM((N_HEADS_A, SUB, tq), F32), pltpu.VMEM((N_HEADS_A, HEAD_DIM_A, tq), F32),
                        pltpu.VMEM((N_HEADS_A, tq // 2, tq), F32), pltpu.VMEM((N_HEADS_A, tq // 2, tq), F32)],
        compiler_params=_params("parallel", "arbitrary"),
        name="dsa_prompt",
    )(qt, qit, wt, kb, vtb, kk)


def _sample_scores_kernel(pt_ref, a_ref, w_ref, kin_ref, *rest, pages):
    page_refs = rest[:pages]
    sp_ref, sn_ref = rest[pages], rest[pages + 1]
    t_new = sn_ref.shape[0]
    a = a_ref[...]
    w = w_ref[...]

    def head_sum(s):
        r = jnp.maximum(s, 0.0) * w
        out = r[0:t_new]
        for h in range(1, N_HEADS_IDX):
            out = out + r[h * t_new:(h + 1) * t_new]
        return out

    for i in range(pages):
        ikt = page_refs[i][...].astype(BF16)
        sp_ref[:, i * PAGE_SIZE:(i + 1) * PAGE_SIZE] = head_sum(_dot(a, ikt))

    @pl.when(pl.program_id(1) == 0)
    def _():
        sn = head_sum(_dot_nt(a, kin_ref[...]))
        ti = lax.broadcasted_iota(jnp.int32, sn.shape, 0)
        si = lax.broadcasted_iota(jnp.int32, sn.shape, 1)
        sn_ref[...] = jnp.where(si <= ti, sn, -jnp.inf)


def _sample_scores(layer, page_table, a_rows, w_rows, ki_new_pad, cache_idx_kt, n_pool, pages):
    batch, n_pages = page_table.shape
    ht = a_rows.shape[1]
    t_new = ht // N_HEADS_IDX
    page_spec = lambda i: pl.BlockSpec(
        (None, D_IDX, PAGE_SIZE), lambda b, s, pt: (layer * n_pool + pt[b, s * pages + i], 0, 0))
    grid_spec = pltpu.PrefetchScalarGridSpec(
        num_scalar_prefetch=1,
        grid=(batch, n_pages // pages),
        in_specs=[pl.BlockSpec((None, ht, D_IDX), lambda b, s, pt: (b, 0, 0)),
                  pl.BlockSpec((None, ht, LANES), lambda b, s, pt: (b, 0, 0)),
                  pl.BlockSpec((None, LANES, D_IDX), lambda b, s, pt: (b, 0, 0))]
        + [page_spec(i) for i in range(pages)],
        out_specs=[pl.BlockSpec((None, t_new, pages * PAGE_SIZE), lambda b, s, pt: (b, 0, s)),
                   pl.BlockSpec((None, t_new, LANES), lambda b, s, pt: (b, 0, 0))],
    )
    return pl.pallas_call(
        functools.partial(_sample_scores_kernel, pages=pages),
        grid_spec=grid_spec,
        out_shape=[jax.ShapeDtypeStruct((batch, t_new, n_pages * PAGE_SIZE), F32),
                   jax.ShapeDtypeStruct((batch, t_new, LANES), F32)],
        compiler_params=_params("parallel", "arbitrary"),
        name="sample_scores",
    )(page_table, a_rows, w_rows, ki_new_pad, *([cache_idx_kt] * pages))


def _sample_select_kernel(sp_ref, sn_ref, selp_ref, seln_ref, sc_ref, lo_ref, hi_ref, flo_ref, *, width, ktop):
    rows, past = sp_ref.shape
    n_tiles = (past + LANES) // width
    for c in range(n_tiles):
        lo, hi = c * width, (c + 1) * width
        if hi <= past:
            sc_ref[c] = sp_ref[:, lo:hi]
        elif lo == past:
            sc_ref[c] = sn_ref[...]
        else:
            sc_ref[c] = jnp.concatenate([sp_ref[:, lo:past], sn_ref[...]], axis=1)

    def get_tile(c):
        return sc_ref[c]

    def set_tile(c, val):
        sc_ref[c] = val

    thr = _topk_threshold(get_tile, set_tile, n_tiles, rows, width, ktop, lo_ref, hi_ref, flo_ref)
    for c in range(n_tiles):
        lo, hi = c * width, (c + 1) * width
        sel = jnp.concatenate([jnp.where(ch >= thr, 1.0, 0.0) for ch in _chunks(sc_ref[c])], axis=1)
        if hi <= past:
            selp_ref[:, lo:hi] = sel
        elif lo == past:
            seln_ref[...] = sel
        else:
            selp_ref[:, lo:past] = sel[:, 0:past - lo]
            seln_ref[...] = sel[:, past - lo:]


def _sample_select(sp, sn, rows, width, ktop):
    n, past = sp.shape
    spec = lambda w: pl.BlockSpec((rows, w), lambda i: (i, 0))
    return pl.pallas_call(
        functools.partial(_sample_select_kernel, width=width, ktop=ktop),
        grid=(n // rows,),
        in_specs=[spec(past), spec(LANES)],
        out_specs=[spec(past), spec(LANES)],
        out_shape=[jax.ShapeDtypeStruct((n, past), F32), jax.ShapeDtypeStruct((n, LANES), F32)],
        scratch_shapes=[pltpu.VMEM(((past + LANES) // width, rows, width), F32), pltpu.VMEM((rows, LANES), F32),
                        pltpu.VMEM((rows, LANES), F32), pltpu.VMEM((rows, LANES), F32)],
        compiler_params=_params("parallel"),
        name="sample_select",
    )(sp, sn)


def _sample_attn_kernel(pt_ref, q_ref, selp_ref, seln_ref, kn_ref, vn_ref, *rest, pages):
    kt_refs = rest[:pages]
    vt_refs = rest[pages:2 * pages]
    o_ref, m_ref, l_ref, acc_ref = rest[2 * pages:]
    step = pl.program_id(1)
    t_new = o_ref.shape[0]
    q = q_ref[...]

    def tile_rows(sel):
        return jnp.concatenate([sel] * N_HEADS_A, axis=0)

    def update(s, pv):
        m_old = m_ref[...]
        m_new = jnp.maximum(m_old, jnp.max(s, axis=1, keepdims=True))
        alpha = jnp.exp2(m_old - m_new)
        p = jnp.exp2(s - m_new)
        l_ref[...] = alpha * l_ref[...] + jnp.sum(p, axis=1, keepdims=True)
        acc_ref[...] = alpha * acc_ref[...] + pv(p.astype(BF16))
        m_ref[...] = m_new

    @pl.when(step == 0)
    def _():
        m_ref[...] = jnp.full(m_ref.shape, NEG, F32)
        l_ref[...] = jnp.zeros(l_ref.shape, F32)
        acc_ref[...] = jnp.zeros(acc_ref.shape, F32)
        s = jnp.where(tile_rows(seln_ref[...]) > 0.5, _dot_nt(q, kn_ref[...]), NEG)
        update(s, lambda p: _dot(p, vn_ref[...]))

    s = jnp.concatenate([_dot(q, kt_refs[i][...].astype(BF16)) for i in range(pages)], axis=1)
    s = jnp.where(tile_rows(selp_ref[...]) > 0.5, s, NEG)

    def pv(p):
        out = _dot_nt(p[:, 0:PAGE_SIZE], vt_refs[0][...].astype(BF16))
        for i in range(1, pages):
            out = out + _dot_nt(p[:, i * PAGE_SIZE:(i + 1) * PAGE_SIZE], vt_refs[i][...].astype(BF16))
        return out

    update(s, pv)

    @pl.when(step == pl.num_programs(1) - 1)
    def _():
        res = acc_ref[...] / l_ref[...]
        lane_head = lax.broadcasted_iota(jnp.int32, (1, D_ATTN), 1) // HEAD_DIM_A
        out = jnp.zeros((t_new, D_ATTN), F32)
        for h in range(N_HEADS_A):
            out = out + jnp.where(lane_head == h, res[h * t_new:(h + 1) * t_new], 0.0)
        o_ref[...] = out.astype(BF16)


def _sample_attn(layer, page_table, q_rows, selp, seln, kn_pad, vn_pad, cache_kt, cache_vt, n_pool, pages):
    batch, n_pages = page_table.shape
    ht = q_rows.shape[1]
    t_new = ht // N_HEADS_A
    page_spec = lambda i: pl.BlockSpec(
        (None, D_ATTN, PAGE_SIZE), lambda b, s, pt: (layer * n_pool + pt[b, s * pages + i], 0, 0))
    per_b = lambda r, w: pl.BlockSpec((None, r, w), lambda b, s, pt: (b, 0, 0))
    grid_spec = pltpu.PrefetchScalarGridSpec(
        num_scalar_prefetch=1,
        grid=(batch, n_pages // pages),
        in_specs=[per_b(ht, D_ATTN),
                  pl.BlockSpec((None, t_new, pages * PAGE_SIZE), lambda b, s, pt: (b, 0, s)),
                  per_b(t_new, LANES), per_b(LANES, D_ATTN), per_b(LANES, D_ATTN)]
        + [page_spec(i) for i in range(pages)] * 2,
        out_specs=per_b(t_new, D_ATTN),
        scratch_shapes=[pltpu.VMEM((ht, 1), F32), pltpu.VMEM((ht, 1), F32), pltpu.VMEM((ht, D_ATTN), F32)],
    )
    return pl.pallas_call(
        functools.partial(_sample_attn_kernel, pages=pages),
        grid_spec=grid_spec,
        out_shape=jax.ShapeDtypeStruct((batch, t_new, D_ATTN), BF16),
        compiler_params=_params("parallel", "arbitrary"),
        name="sample_attn",
    )(page_table, q_rows, selp, seln, kn_pad, vn_pad, *([cache_kt] * pages), *([cache_vt] * pages))


def _merge_kernel(x_ref, a_ref, g_ref, ga_ref, gb_ref, wa_ref, wb_ref, wo_ref, n2_ref, x1_ref, h2_ref, *, a_key_major):
    a_proj = _dot_tn(a_ref[...], wa_ref[...]) if a_key_major else _dot(a_ref[...], wa_ref[...])
    mixed = (jax.nn.sigmoid(ga_ref[...]) * a_proj
             + jax.nn.sigmoid(gb_ref[...]) * _dot(g_ref[...], wb_ref[...]))
    x1 = x_ref[...] + _dot(mixed.astype(BF16), wo_ref[...])
    x1_ref[...] = x1
    h2_ref[...] = (x1 * lax.rsqrt(jnp.mean(x1 * x1, axis=-1, keepdims=True) + EPS) * n2_ref[...]).astype(BF16)


def _merge(x2d, a_o, g_o, ga, gb, w_a, w_b, w_o, norm2, batch, tm, a_key_major):
    n, d = x2d.shape
    nt = n // batch // tm
    row = lambda w: pl.BlockSpec((tm, w), lambda b, i: (b * nt + i, 0))
    a_spec = pl.BlockSpec((None, D_ATTN, tm), lambda b, i: (b, 0, i)) if a_key_major else row(D_ATTN)
    weights = (w_a.astype(BF16), w_b.astype(BF16), w_o.astype(BF16))
    return pl.pallas_call(
        functools.partial(_merge_kernel, a_key_major=a_key_major),
        grid=(batch, nt),
        in_specs=[row(d), a_spec, row(D_GLA_V), row(d), row(d)]
        + [_const_spec(w.shape) for w in weights] + [_const_spec((1, d))],
        out_specs=[row(d), row(d)],
        out_shape=[jax.ShapeDtypeStruct((n, d), F32), jax.ShapeDtypeStruct((n, d), BF16)],
        compiler_params=_params("parallel", "parallel"),
        name="merge",
    )(x2d, a_o, g_o, ga, gb, *weights, norm2.reshape(1, d))


def _ffn_kernel(x1_ref, h2_ref, wg_ref, wu_ref, wd_ref, nf_ref, y_ref, *, n_split, final_norm):
    h2 = h2_ref[...]
    y = x1_ref[...]
    fc = wg_ref.shape[1] // n_split
    for i in range(n_split):
        g = _dot(h2, wg_ref[:, i * fc:(i + 1) * fc])
        u = _dot(h2, wu_ref[:, i * fc:(i + 1) * fc])
        act = (g * jax.nn.sigmoid(g) * u).astype(BF16)
        y = y + _dot(act, wd_ref[i * fc:(i + 1) * fc, :])
    if final_norm:
        y = y * lax.rsqrt(jnp.mean(y * y, axis=-1, keepdims=True) + EPS) * nf_ref[...]
    y_ref[...] = y


def _ffn(x1, h2, w_g, w_u, w_d, norm_final, tm, final_norm):
    n, d = x1.shape
    d_ff = w_g.shape[1]
    n_split = 2 if d_ff % (2 * LANES) == 0 else 1
    row = lambda: pl.BlockSpec((tm, d), lambda i: (i, 0))
    weights = (w_g.astype(BF16), w_u.astype(BF16), w_d.astype(BF16))
    return pl.pallas_call(
        functools.partial(_ffn_kernel, n_split=n_split, final_norm=final_norm),
        grid=(n // tm,),
        in_specs=[row(), row()] + [_const_spec(w.shape) for w in weights] + [_const_spec((1, d))],
        out_specs=row(),
        out_shape=jax.ShapeDtypeStruct((n, d), F32),
        compiler_params=_params("parallel"),
        name="ffn",
    )(x1, h2, *weights, norm_final.reshape(1, d))


def _pick(n, prefs):
    for p in prefs:
        if n % p == 0:
            return p
    return n


def kernel(x_prompt, x_sample, cache_k, cache_v, cache_idx_k, state_gla, page_table, norm1, w_in, w_gate_up,
           b_gate, g_gla_norm, w_branch_a, w_branch_b, w_out, norm2, w_ffn_gate, w_ffn_up, w_ffn_down, norm_final):
    depth = w_in.shape[0]
    bp, seq, d = x_prompt.shape
    bs, t_new, _ = x_sample.shape
    n_pool = cache_k.shape[1]
    n_pages = page_table.shape[1]
    past = n_pages * PAGE_SIZE
    xp = x_prompt.reshape(bp * seq, d)
    xs = x_sample.reshape(bs * t_new, d)
    cache_kt = cache_k.transpose(0, 1, 3, 4, 2).reshape(depth * n_pool, D_ATTN, PAGE_SIZE)
    cache_vt = cache_v.transpose(0, 1, 3, 4, 2).reshape(depth * n_pool, D_ATTN, PAGE_SIZE)
    cache_ikt = cache_idx_k.transpose(0, 1, 3, 2).reshape(depth * n_pool, D_IDX, PAGE_SIZE)
    outs = [[] for _ in range(8)]
    tq = _pick(seq, (256, 128))
    for l in range(depth):
        last = l == depth - 1
        (qkg, vg, la, r, ga, gb, qt, qit, wt, kb, kk, kt, vt, vtb, kit) = _inproj(
            xp, norm1[l], w_in[l], w_gate_up[l], b_gate[l], bp, _pick(seq, (256, 128)), True)
        a_o = _dsa_prompt(qt, qit, wt, kb, vtb, kk, bp, seq, tq)
        chunk = GLA_CHUNK if seq % GLA_CHUNK == 0 else seq
        g_o, st = _gla(qkg, vg, la, r, g_gla_norm[l], jnp.zeros((bp, DV_G, D_GLA_K), F32), bp, seq, chunk,
                       _pick(seq, (512, 256, 128, 64)), _pick(bp, (2, 1)))
        x1, h2 = _merge(xp, a_o, g_o, ga, gb, w_branch_a[l], w_branch_b[l], w_out[l], norm2[l], bp,
                        _pick(seq, (512, 256, 128)), True)
        xp = _ffn(x1, h2, w_ffn_gate[l], w_ffn_up[l], w_ffn_down[l], norm_final, _pick(bp * seq, (256, 128)), last)
        outs[0].append(kt.reshape(bp, N_HEADS_A, HEAD_DIM_A, seq).transpose(0, 3, 1, 2))
        outs[1].append(vt.reshape(bp, N_HEADS_A, HEAD_DIM_A, seq).transpose(0, 3, 1, 2))
        outs[2].append(kit.transpose(0, 2, 1))
        outs[3].append(_state_from_t(st))

        (qkg, vg, la, r, ga, gb, q, qi, kw, k, v, kb, vb, kk) = _inproj(
            xs, norm1[l], w_in[l], w_gate_up[l], b_gate[l], 1, _pick(bs * t_new, (256, 128)), False)
        a_rows = qi.reshape(bs, t_new, N_HEADS_IDX, D_IDX).transpose(0, 2, 1, 3).reshape(bs, N_HEADS_IDX * t_new, D_IDX)
        w_rows = (kw[:, D_IDX:D_IDX + N_HEADS_IDX] * (N_HEADS_IDX ** -0.5 * D_IDX ** -0.5)).reshape(
            bs, t_new, N_HEADS_IDX).transpose(0, 2, 1).reshape(bs, N_HEADS_IDX * t_new, 1)
        w_rows = jnp.broadcast_to(w_rows, (bs, N_HEADS_IDX * t_new, LANES))
        pad_rows = lambda a: jnp.pad(a.reshape(bs, t_new, a.shape[-1]), ((0, 0), (0, LANES - t_new), (0, 0)))
        sp, sn = _sample_scores(l, page_table, a_rows, w_rows, pad_rows(kk[:, 0:D_IDX]), cache_ikt, n_pool,
                                _pick(n_pages, (16, 8, 4, 2, 1)))
        ktop = min(TOPK_MAX, (past + t_new) // 4)
        width = 5 * LANES if (past + LANES) % (5 * LANES) == 0 else LANES
        rows = _pick(bs * t_new, (128, 64, 32, 16, 8))
        selp, seln = _sample_select(sp.reshape(bs * t_new, past), sn.reshape(bs * t_new, LANES), rows, width, ktop)
        head_of_lane = jnp.arange(D_ATTN) // HEAD_DIM_A
        q_rows = jnp.where(head_of_lane[None, None, None, :] == jnp.arange(N_HEADS_A)[None, :, None, None],
                           q.reshape(bs, 1, t_new, D_ATTN), jnp.zeros((), BF16)).reshape(bs, N_HEADS_A * t_new, D_ATTN)
        a_o = _sample_attn(l, page_table, q_rows, selp.reshape(bs, t_new, past), seln.reshape(bs, t_new, LANES),
                           pad_rows(kb), pad_rows(vb), cache_kt, cache_vt, n_pool, _pick(n_pages, (16, 8, 4, 2, 1)))
        gchunk = 16
        pad_g = lambda a: jnp.pad(a.reshape(bs, t_new, a.shape[-1]), ((0, 0), (0, gchunk - t_new), (0, 0))).reshape(
            bs * gchunk, a.shape[-1])
        g_o, st = _gla(pad_g(qkg), pad_g(vg), pad_g(la), pad_g(r), g_gla_norm[l], _state_to_t(state_gla[l].astype(F32)),
                       bs, gchunk, gchunk, gchunk, _pick(bs, (8, 4, 2, 1)))
        g_o = g_o.reshape(bs, gchunk, D_GLA_V)[:, 0:t_new].reshape(bs * t_new, D_GLA_V)
        x1, h2 = _merge(xs, a_o.reshape(bs * t_new, D_ATTN), g_o, ga, gb, w_branch_a[l], w_branch_b[l], w_out[l],
                        norm2[l], 1, _pick(bs * t_new, (512, 256, 128)), False)
        xs = _ffn(x1, h2, w_ffn_gate[l], w_ffn_up[l], w_ffn_down[l], norm_final, _pick(bs * t_new, (256, 128)), last)
        outs[4].append(k.reshape(bs, t_new, N_HEADS_A, HEAD_DIM_A))
        outs[5].append(v.reshape(bs, t_new, N_HEADS_A, HEAD_DIM_A))
        outs[6].append(kw[:, 0:D_IDX].reshape(bs, t_new, D_IDX))
        outs[7].append(_state_from_t(st))
    return (xp.reshape(bp, seq, d), xs.reshape(bs, t_new, d)) + tuple(jnp.stack(o) for o in outs)
```

```python
import functools

import jax
import jax.numpy as jnp
from jax import lax
from jax.experimental import pallas as pl
from jax.experimental.pallas import tpu as pltpu

F32 = jnp.float32
BF16 = jnp.bfloat16

D_ATTN = 512
N_HEADS_A = 8
HEAD_DIM_A = 64
N_HEADS_IDX = 8
D_IDX = 64
TOPK_MAX = 256
N_HEADS_G = 4
DK_G = 64
DV_G = 128
D_GLA_K = N_HEADS_G * DK_G
D_GLA_V = N_HEADS_G * DV_G
GATE_RANK = 16
GATE_TAU = 16.0
GLA_CHUNK = 64
EPS = 1e-6
PAGE_SIZE = 128

LANES = 128
NEG = -1e30
LOWEST = -3.0e38
MAX_BISECT = 64
Q_SCALE = HEAD_DIM_A ** -0.5 * 1.4426950408889634
VMEM_LIMIT = 56 * 1024 * 1024


def _dot(a, b):
    return jnp.dot(a, b, preferred_element_type=F32)


def _dot_nt(a, b):
    return lax.dot_general(a, b, (((1,), (1,)), ((), ())), preferred_element_type=F32)


def _dot_tn(a, b):
    return lax.dot_general(a, b, (((0,), (0,)), ((), ())), preferred_element_type=F32)


def _params(*sem):
    return pltpu.CompilerParams(dimension_semantics=sem, vmem_limit_bytes=VMEM_LIMIT)


def _const_spec(shape):
    n = len(shape)
    return pl.BlockSpec(shape, lambda *_: (0,) * n)


def _chunks(x):
    return [x[:, i * LANES:(i + 1) * LANES] for i in range(x.shape[1] // LANES)]


def _rep(col, rows):
    return jnp.broadcast_to(col, (rows, LANES))


def _inproj_kernel(x_ref, g_ref, wc_ref, wd_ref, we_ref, wgu_ref, bg_ref, *refs, key_major):
    x = x_ref[...]
    h = (x * lax.rsqrt(jnp.mean(x * x, axis=-1, keepdims=True) + EPS) * g_ref[...]).astype(BF16)
    if key_major:
        (wqt_ref, wkvt_ref, wk_ref, wkk_ref, wkit_ref, wwit_ref,
         qkg_ref, vg_ref, la_ref, r_ref, ga_ref, gb_ref,
         qt_ref, qit_ref, wt_ref, kb_ref, kk_ref, kt_ref, vt_ref, vtb_ref, kit_ref) = refs
        qq = _dot_nt(wqt_ref[...], h)
        qt_ref[...] = (qq[0:512] * Q_SCALE).astype(BF16)
        qit_ref[...] = qq[512:1024].astype(BF16)
        kv = _dot_nt(wkvt_ref[...], h)
        kt_ref[...] = kv[0:512]
        vt_ref[...] = kv[512:1024]
        vtb_ref[...] = kv[512:1024].astype(BF16)
        kb_ref[...] = _dot(h, wk_ref[...]).astype(BF16)
        kk_ref[...] = _dot(h, wkk_ref[...]).astype(BF16)
        kit_ref[...] = _dot_nt(wkit_ref[...], h)
        wt_ref[...] = _dot_nt(wwit_ref[...], h)[0:N_HEADS_IDX] * (N_HEADS_IDX ** -0.5 * D_IDX ** -0.5)
    else:
        (wq_ref, wkv_ref, wkk_ref, wkw_ref,
         qkg_ref, vg_ref, la_ref, r_ref, ga_ref, gb_ref,
         q_ref, qi_ref, kw_ref, k_ref, v_ref, kb_ref, vb_ref, kk_ref) = refs
        a = _dot(h, wq_ref[...])
        q_ref[...] = (a[:, 0:512] * Q_SCALE).astype(BF16)
        qi_ref[...] = a[:, 512:1024].astype(BF16)
        kw_ref[...] = _dot(h, wkw_ref[...])
        kv = _dot(h, wkv_ref[...])
        k_ref[...] = kv[:, 0:512]
        v_ref[...] = kv[:, 512:1024]
        kb_ref[...] = kv[:, 0:512].astype(BF16)
        vb_ref[...] = kv[:, 512:1024].astype(BF16)
        kk_ref[...] = _dot(h, wkk_ref[...]).astype(BF16)
    c = _dot(h, wc_ref[...])
    qkg_ref[...] = c[:, 0:512]
    vg_ref[...] = c[:, 512:1024]
    glr = _dot(h, wd_ref[...]).astype(BF16)
    z = _dot(glr, wgu_ref[...]) + bg_ref[...]
    la_ref[...] = (jnp.minimum(z, 0.0) - jnp.log1p(jnp.exp(-jnp.abs(z)))) * (1.0 / GATE_TAU)
    e = _dot(h, we_ref[...])
    r_ref[...] = e[:, 0:512]
    ga_ref[...] = e[:, 512:1536]
    gb_ref[...] = e[:, 1536:2560]


def _inproj(x2d, norm_g, w_in, w_gate_up, b_gate, batch, tm, key_major):
    n, d = x2d.shape
    t_len = n // batch
    nt = t_len // tm
    wb = w_in.astype(BF16)
    wq = jnp.concatenate([wb[:, 0:512], wb[:, 1536:2048]], axis=1)
    wkv = wb[:, 512:1536]
    wki = wb[:, 2048:2112]
    wwi = wb[:, 2112:2120]
    wkk = jnp.concatenate([wki, wki], axis=1)
    wc = wb[:, 2120:3144]
    wd = jnp.concatenate([wb[:, 3144:3160], jnp.zeros((d, LANES - GATE_RANK), BF16)], axis=1)
    we = wb[:, 3160:5720]
    wgu = jnp.concatenate([w_gate_up.astype(BF16), jnp.zeros((LANES - GATE_RANK, D_GLA_K), BF16)], axis=0)
    row = lambda w: pl.BlockSpec((tm, w), lambda b, i: (b * nt + i, 0))
    col = lambda r: pl.BlockSpec((None, r, tm), lambda b, i: (b, 0, i))
    rows_out = lambda outs: ([row(w) for w, _ in outs], [jax.ShapeDtypeStruct((n, w), dt) for w, dt in outs])
    cols_out = lambda outs: ([col(r) for r, _ in outs],
                             [jax.ShapeDtypeStruct((batch, r, t_len), dt) for r, dt in outs])
    out_specs, out_shape = rows_out([(512, F32), (512, F32), (256, F32), (512, F32), (1024, F32), (1024, F32)])
    if key_major:
        wwit = jnp.concatenate([wwi.T, jnp.zeros((16 - N_HEADS_IDX, d), BF16)], axis=0)
        flavor_w = (wq.T, wkv.T, wb[:, 512:1024], wkk, wki.T, wwit)
        s1, h1 = cols_out([(512, BF16), (512, BF16), (N_HEADS_IDX, F32)])
        s2, h2 = rows_out([(512, BF16), (LANES, BF16)])
        s3, h3 = cols_out([(512, F32), (512, F32), (512, BF16), (D_IDX, F32)])
        out_specs, out_shape = out_specs + s1 + s2 + s3, out_shape + h1 + h2 + h3
    else:
        wkw = jnp.concatenate([wki, wwi, jnp.zeros((d, LANES - 72), BF16)], axis=1)
        flavor_w = (wq, wkv, wkk, wkw)
        s1, h1 = rows_out([(512, BF16), (512, BF16), (LANES, F32), (512, F32), (512, F32), (512, BF16),
                           (512, BF16), (LANES, BF16)])
        out_specs, out_shape = out_specs + s1, out_shape + h1
    weights = (wc, wd, we, wgu)
    return pl.pallas_call(
        functools.partial(_inproj_kernel, key_major=key_major),
        grid=(batch, nt),
        in_specs=[row(d), _const_spec((1, d))] + [_const_spec(w.shape) for w in weights]
        + [_const_spec((1, D_GLA_K))] + [_const_spec(w.shape) for w in flavor_w],
        out_specs=out_specs,
        out_shape=out_shape,
        compiler_params=_params("parallel", "parallel"),
        name="inproj",
    )(x2d, norm_g.reshape(1, d), *weights, b_gate.reshape(1, D_GLA_K), *flavor_w)


def _gla_kernel(qk_ref, v_ref, la_ref, r_ref, gn_ref, s0_ref, go_ref, sfin_ref, st_ref, *, chunk, n_chunks):
    step = pl.program_id(1)

    @pl.when(step == 0)
    def _():
        st_ref[...] = s0_ref[...]

    ri = lax.broadcasted_iota(jnp.int32, (chunk, chunk), 0)
    ci = lax.broadcasted_iota(jnp.int32, (chunk, chunk), 1)
    tril = ci <= ri
    tri = jnp.where(tril, 1.0, 0.0).astype(F32)
    lane_head = lax.broadcasted_iota(jnp.int32, (1, D_GLA_K), 1) // DK_G
    mid = chunk // 2 - 1
    gn = gn_ref[...]

    def body(c, carry):
        for g in range(st_ref.shape[0]):
            one_chunk(c, g)
        return carry

    def one_chunk(c, g):
        r0 = pl.multiple_of(c * chunk, chunk)
        qk = qk_ref[g, pl.ds(r0, chunk), :]
        q = qk[:, 0:D_GLA_K] * DK_G ** -0.5
        k = qk[:, D_GLA_K:2 * D_GLA_K]
        la = la_ref[g, pl.ds(r0, chunk), :]
        b = jnp.dot(tri, la, precision=lax.Precision.HIGHEST, preferred_element_type=F32)
        b_mid = b[mid:mid + 1, :]
        b_end = b[chunk - 1:chunk, :]
        qe = q * jnp.exp(b)
        qt = q * jnp.exp(b - b_mid)
        ktb = (k * jnp.exp(b_mid - b)).astype(BF16)
        kdb = (k * jnp.exp(b_end - b)).astype(BF16)
        st = st_ref[g]
        stb = st.astype(BF16)
        new_st = st * jnp.exp(b_end)
        for h in range(N_HEADS_G):
            hm = lane_head == h
            qt_h = jnp.where(hm, qt, 0.0).astype(BF16)
            qe_h = jnp.where(hm, qe, 0.0).astype(BF16)
            att = jnp.where(tril, _dot_nt(qt_h, ktb), 0.0)
            vb = v_ref[g, pl.ds(r0, chunk), h * DV_G:(h + 1) * DV_G].astype(BF16)
            o = _dot_nt(qe_h, stb) + _dot(att.astype(BF16), vb)
            new_st = new_st + jnp.where(hm, _dot_tn(vb, kdb), 0.0)
            on = o * lax.rsqrt(jnp.mean(o * o, axis=-1, keepdims=True) + EPS) * gn
            rr = r_ref[g, pl.ds(r0, chunk), h * DV_G:(h + 1) * DV_G]
            go_ref[g, pl.ds(r0, chunk), h * DV_G:(h + 1) * DV_G] = (on * (rr * jax.nn.sigmoid(rr))).astype(BF16)
        st_ref[g] = new_st

    lax.fori_loop(0, n_chunks, body, 0)

    @pl.when(step == pl.num_programs(1) - 1)
    def _():
        sfin_ref[...] = st_ref[...]


def _gla(qkg, vg, la, r, g_norm, s0_t, batch, t_len, chunk, rows_per_step, group):
    n_steps = t_len // rows_per_step
    row = lambda w: pl.BlockSpec((group, rows_per_step, w), lambda b, s: (b, s, 0))
    st_spec = pl.BlockSpec((group, DV_G, D_GLA_K), lambda b, s: (b, 0, 0))
    seqs = lambda a: a.reshape(batch, t_len, a.shape[-1])
    g_o, st = pl.pallas_call(
        functools.partial(_gla_kernel, chunk=chunk, n_chunks=rows_per_step // chunk),
        grid=(batch // group, n_steps),
        in_specs=[row(512), row(512), row(256), row(512), _const_spec((1, DV_G)), st_spec],
        out_specs=[row(512), st_spec],
        out_shape=[jax.ShapeDtypeStruct((batch, t_len, D_GLA_V), BF16),
                   jax.ShapeDtypeStruct((batch, DV_G, D_GLA_K), F32)],
        scratch_shapes=[pltpu.VMEM((group, DV_G, D_GLA_K), F32)],
        compiler_params=_params("parallel", "arbitrary"),
        name="gla",
    )(seqs(qkg), seqs(vg), seqs(la), seqs(r), g_norm.reshape(1, DV_G), s0_t)
    return g_o.reshape(batch * t_len, D_GLA_V), st


def _state_to_t(s):
    b = s.shape[0]
    return s.transpose(0, 3, 1, 2).reshape(b, DV_G, D_GLA_K)


def _state_from_t(st):
    b = st.shape[0]
    return st.reshape(b, DV_G, N_HEADS_G, DK_G).transpose(0, 2, 3, 1)


def _topk_threshold(get_tile, set_tile, n_tiles, rows, width, ktop, lo_ref, hi_ref, flo_ref):
    kf = float(ktop)
    one = lambda m: jnp.where(m, 1.0, 0.0)
    zeros = jnp.zeros((rows, LANES), F32)
    rsum = lambda a: _rep(jnp.sum(a, axis=1, keepdims=True), rows)

    def acc_chunks(blk, acc, fn, op):
        for ch in _chunks(blk):
            acc = op(acc, fn(ch))
        return acc

    def stats(c, st):
        mx, mn, nv, c0, c0e = st
        blk = get_tile(c)
        mx = acc_chunks(blk, mx, lambda ch: ch, jnp.maximum)
        mn = acc_chunks(blk, mn, lambda ch: jnp.where(ch > -jnp.inf, ch, jnp.inf), jnp.minimum)
        nv = acc_chunks(blk, nv, lambda ch: one(ch > -jnp.inf), jnp.add)
        c0 = acc_chunks(blk, c0, lambda ch: one(ch > 0.0), jnp.add)
        c0e = acc_chunks(blk, c0e, lambda ch: one(ch >= 0.0), jnp.add)
        return mx, mn, nv, c0, c0e

    mx, mn, nv, c0, c0e = lax.fori_loop(
        0, n_tiles, stats,
        (jnp.full((rows, LANES), -jnp.inf, F32), jnp.full((rows, LANES), jnp.inf, F32), zeros, zeros, zeros))
    rowmax = _rep(jnp.max(mx, axis=1, keepdims=True), rows)
    rowmin = _rep(jnp.min(mn, axis=1, keepdims=True), rows)
    nvalid, cnt0, cnt0e = rsum(nv), rsum(c0), rsum(c0e)

    def count_gt(thr):
        def body(c, acc):
            return acc_chunks(get_tile(c), acc, lambda ch: one(ch > thr), jnp.add)
        return rsum(lax.fori_loop(0, n_tiles, body, zeros))

    small = nvalid <= kf
    ztie = jnp.logical_and(jnp.logical_and(cnt0 < kf, cnt0e >= kf), jnp.logical_not(small))
    frozen = jnp.logical_or(small, ztie)
    pos = cnt0 >= kf
    lo_ref[...] = jnp.where(frozen, 0.0, jnp.where(pos, 0.0, jnp.minimum(rowmin - 1.0, rowmin * 2.0)))
    hi_ref[...] = jnp.where(frozen, 0.0, jnp.where(pos, rowmax, 0.0))
    flo_ref[...] = jnp.where(frozen, kf, jnp.where(pos, cnt0, nvalid))

    def cond(st):
        it, pending = st
        return jnp.logical_and(it < MAX_BISECT, pending > 0)

    def bisect(st):
        it, _ = st
        lo, hi = lo_ref[...], hi_ref[...]
        mid = 0.5 * (lo + hi)
        c = count_gt(mid)
        ge = c >= kf
        flo = jnp.where(ge, c, flo_ref[...])
        lo_ref[...] = jnp.where(ge, mid, lo)
        hi_ref[...] = jnp.where(ge, hi, mid)
        flo_ref[...] = flo
        return it + 1, jnp.max(one(flo != kf)).astype(jnp.int32)

    pending0 = jnp.max(one(flo_ref[...] != kf)).astype(jnp.int32)
    lax.while_loop(cond, bisect, (jnp.int32(0), pending0))

    lo = lo_ref[...]

    def tmin(c, acc):
        return acc_chunks(get_tile(c), acc, lambda ch: jnp.where(ch > lo, ch, jnp.inf), jnp.minimum)

    thr = _rep(jnp.min(lax.fori_loop(0, n_tiles, tmin, jnp.full((rows, LANES), jnp.inf, F32)),
                       axis=1, keepdims=True), rows)
    thr = jnp.where(ztie, 0.0, thr)
    thr = jnp.where(small, LOWEST, thr)

    def gteq(c, st):
        g, e = st
        blk = get_tile(c)
        return (acc_chunks(blk, g, lambda ch: one(ch > thr), jnp.add),
                acc_chunks(blk, e, lambda ch: one(ch == thr), jnp.add))

    g, e = lax.fori_loop(0, n_tiles, gteq, (zeros, zeros))
    need = kf - rsum(g)
    any_tie = jnp.max(one(rsum(e) > need)) > 0.0

    @pl.when(any_tie)
    def _():
        ri = lax.broadcasted_iota(jnp.int32, (width, width), 0)
        ci = lax.broadcasted_iota(jnp.int32, (width, width), 1)
        upper = jnp.where(ri <= ci, 1.0, 0.0).astype(BF16)
        thr_c, need_c = thr[:, 0:1], need[:, 0:1]

        def drop(c, seen):
            blk = get_tile(c)
            eq = blk == thr_c
            eqf = one(eq)
            rank = _dot(eqf.astype(BF16), upper) + seen - 1.0
            set_tile(c, jnp.where(jnp.logical_and(eq, rank >= need_c), -jnp.inf, blk))
            return seen + jnp.sum(eqf, axis=1, keepdims=True)

        lax.fori_loop(0, n_tiles, drop, jnp.zeros((rows, 1), F32))

    return thr


SUB = 8


def _groups(x):
    return [x[i * SUB:(i + 1) * SUB] for i in range(x.shape[0] // SUB)]


def _tree(op, xs):
    xs = list(xs)
    while len(xs) > 1:
        xs = [op(xs[i], xs[i + 1]) if i + 1 < len(xs) else xs[i] for i in range(0, len(xs), 2)]
    return xs[0]


def _col(op, x8):
    return jnp.broadcast_to(op(x8, axis=0, keepdims=True), x8.shape)


def _col_stats_init(n):
    zeros = jnp.zeros((SUB, n), F32)
    return jnp.full((SUB, n), -jnp.inf, F32), jnp.full((SUB, n), jnp.inf, F32), zeros, zeros, zeros


def _col_stats_update(st, tile):
    one = lambda m: jnp.where(m, 1.0, 0.0)
    mx, mn, nv, c0, c0e = st
    g = _groups(tile)
    mx = jnp.maximum(mx, _tree(jnp.maximum, g))
    mn = jnp.minimum(mn, _tree(jnp.minimum, [jnp.where(x > -jnp.inf, x, jnp.inf) for x in g]))
    nv = nv + _tree(jnp.add, [one(x > -jnp.inf) for x in g])
    c0 = c0 + _tree(jnp.add, [one(x > 0.0) for x in g])
    c0e = c0e + _tree(jnp.add, [one(x >= 0.0) for x in g])
    return mx, mn, nv, c0, c0e


def _topk_cut_cols(get_tile, set_tile, n_tiles, tk, n, ktop, stats, lo_ref, hi_ref, flo_ref):
    kf = float(ktop)
    one = lambda m: jnp.where(m, 1.0, 0.0)
    zeros = jnp.zeros((SUB, n), F32)
    total = lambda a: _col(jnp.sum, a)
    mx, mn, nv, c0, c0e = stats
    colmax, colmin = _col(jnp.max, mx), _col(jnp.min, mn)
    nvalid, cnt0, cnt0e = total(nv), total(c0), total(c0e)

    def count_gt(thr):
        def body(c, acc):
            return acc + _tree(jnp.add, [one(x > thr) for x in _groups(get_tile(c))])
        return total(lax.fori_loop(0, n_tiles, body, zeros))

    small = nvalid <= kf
    ztie = jnp.logical_and(jnp.logical_and(cnt0 < kf, cnt0e >= kf), jnp.logical_not(small))
    frozen = jnp.logical_or(small, ztie)
    pos = cnt0 >= kf
    lo_ref[...] = jnp.where(frozen, 0.0, jnp.where(pos, 0.0, jnp.minimum(colmin - 1.0, colmin * 2.0)))
    hi_ref[...] = jnp.where(frozen, 0.0, jnp.where(pos, colmax, 0.0))
    flo_ref[...] = jnp.where(frozen, kf, jnp.where(pos, cnt0, nvalid))

    def cond(st):
        it, pending = st
        return jnp.logical_and(it < MAX_BISECT, pending > 0)

    def bisect(st):
        it, _ = st
        lo, hi = lo_ref[...], hi_ref[...]
        mid = 0.5 * (lo + hi)
        c = count_gt(mid)
        ge = c >= kf
        flo = jnp.where(ge, c, flo_ref[...])
        lo_ref[...] = jnp.where(ge, mid, lo)
        hi_ref[...] = jnp.where(ge, hi, mid)
        flo_ref[...] = flo
        return it + 1, jnp.max(one(flo != kf)).astype(jnp.int32)

    pending0 = jnp.max(one(flo_ref[...] != kf)).astype(jnp.int32)
    lax.while_loop(cond, bisect, (jnp.int32(0), pending0))

    lo = lo_ref[...]
    unresolved = jnp.logical_or(ztie, flo_ref[...] != kf)
    lo_ref[...] = jnp.where(small, -jnp.inf, lo)

    @pl.when(jnp.max(one(unresolved)) > 0.0)
    def _():
        def tmin(c, acc):
            return jnp.minimum(
                acc, _tree(jnp.minimum, [jnp.where(x > lo, x, jnp.inf) for x in _groups(get_tile(c))]))

        thr = _col(jnp.min, lax.fori_loop(0, n_tiles, tmin, jnp.full((SUB, n), jnp.inf, F32)))
        thr = jnp.where(ztie, 0.0, thr)

        def gt(c, g):
            return g + _tree(jnp.add, [one(x > thr) for x in _groups(get_tile(c))])

        need = kf - total(lax.fori_loop(0, n_tiles, gt, zeros))
        ri = lax.broadcasted_iota(jnp.int32, (tk, tk), 0)
        ci = lax.broadcasted_iota(jnp.int32, (tk, tk), 1)
        lower = jnp.where(ci <= ri, 1.0, 0.0).astype(BF16)
        thr_r, need_r, unres_r = thr[0:1], need[0:1], unresolved[0:1]

        def drop(c, st):
            seen, below = st
            blk = get_tile(c)
            eq = jnp.logical_and(blk == thr_r, unres_r)
            eqf = one(eq)
            rank = _dot(lower, eqf.astype(BF16)) + seen - 1.0
            set_tile(c, jnp.where(jnp.logical_and(eq, rank >= need_r), -jnp.inf, blk))
            below = jnp.maximum(below, jnp.max(jnp.where(blk < thr_r, blk, -jnp.inf), axis=0, keepdims=True))
            return seen + jnp.sum(eqf, axis=0, keepdims=True), below

        _, below = lax.fori_loop(0, n_tiles, drop, (jnp.zeros((1, n), F32), jnp.full((1, n), -jnp.inf, F32)))
        lo_ref[...] = jnp.where(unresolved, jnp.broadcast_to(below, (SUB, n)), lo_ref[...])

    return lo_ref[...]


def _dsa_prompt_kernel(qt_ref, qit_ref, wt_ref, k_ref, vt_ref, kk_ref, o_ref,
                       sc_ref, lo_ref, hi_ref, flo_ref, m_ref, l_ref, acc_ref, sa_ref, sb_ref, *, tq, ktop):
    j = pl.program_id(1)
    n_tiles = j + 1
    t0 = j * tq
    qpos = t0 + lax.broadcasted_iota(jnp.int32, (1, tq), 1)
    kiota = lax.broadcasted_iota(jnp.int32, (tq, 1), 0)
    per = LANES // HEAD_DIM_A
    reps = tq // SUB
    zero_rows = jnp.zeros((LANES - D_IDX, tq), BF16)

    def tall(x8):
        return pltpu.repeat(x8, reps, axis=0)

    qit = qit_ref[...]
    wt = wt_ref[...]
    qih = [jnp.concatenate([qit[h * D_IDX:(h + 1) * D_IDX], zero_rows], axis=0) for h in range(N_HEADS_IDX)]
    wrow = [tall(jnp.broadcast_to(wt[h:h + 1], (SUB, tq))) for h in range(N_HEADS_IDX)]

    def score_tile(c, stats):
        k0 = pl.multiple_of(c * tq, tq)
        kk = kk_ref[pl.ds(k0, tq), :]
        acc = jnp.zeros((tq, tq), F32)
        for h in range(N_HEADS_IDX):
            acc = acc + jnp.maximum(_dot(kk, qih[h]), 0.0) * wrow[h]
        tile = jnp.where(k0 + kiota <= qpos, acc, -jnp.inf)
        sc_ref[c] = tile
        return _col_stats_update(stats, tile)

    stats = lax.fori_loop(0, n_tiles, score_tile, _col_stats_init(tq))

    def get_tile(c):
        return sc_ref[c]

    def set_tile(c, val):
        sc_ref[c] = val

    cut = tall(_topk_cut_cols(get_tile, set_tile, n_tiles, tq, tq, ktop, stats, lo_ref, hi_ref, flo_ref))

    qt = qt_ref[...]
    zq = jnp.zeros((HEAD_DIM_A, tq), BF16)
    qh = []
    for h in range(N_HEADS_A):
        rows = qt[h * HEAD_DIM_A:(h + 1) * HEAD_DIM_A]
        qh.append(jnp.concatenate([zq] * (h % per) + [rows] + [zq] * (per - 1 - h % per), axis=0))
    m_ref[...] = jnp.full(m_ref.shape, NEG, F32)
    l_ref[...] = jnp.zeros(l_ref.shape, F32)
    acc_ref[...] = jnp.zeros(acc_ref.shape, F32)

    ta = tq // 2
    cut_a = cut[0:ta]
    ones_rows = jnp.ones((2 * SUB, ta), BF16)
    last = 2 * n_tiles - 1

    def logits(c, buf):
        k0 = pl.multiple_of(c * ta, ta)
        for h in range(N_HEADS_A):
            buf[h] = _dot(k_ref[pl.ds(k0, ta), (h // per) * LANES:(h // per + 1) * LANES], qh[h])

    def softmax_pv(c, half, buf):
        k0 = pl.multiple_of(c * ta, ta)
        bias = jnp.where(sc_ref[c // 2, half * ta:(half + 1) * ta, :] > cut_a, 0.0, NEG)
        for h in range(N_HEADS_A):
            s = buf[h] + bias
            vh = jnp.concatenate([vt_ref[h * HEAD_DIM_A:(h + 1) * HEAD_DIM_A, pl.ds(k0, ta)], ones_rows], axis=0)
            m_old = m_ref[h]
            m_new = jnp.maximum(m_old, _col(jnp.max, _tree(jnp.maximum, _groups(s))))
            alpha = jnp.exp2(m_old - m_new)
            p = jnp.exp2(s - pltpu.repeat(m_new, ta // SUB, axis=0))
            pv = _dot(vh, p.astype(BF16))
            l_ref[h] = alpha * l_ref[h] + pv[HEAD_DIM_A:HEAD_DIM_A + SUB]
            acc_ref[h] = pltpu.repeat(alpha, HEAD_DIM_A // SUB, axis=0) * acc_ref[h] + pv[0:HEAD_DIM_A]
            m_ref[h] = m_new

    logits(0, sa_ref)

    def att_tile(i, carry):
        logits(2 * i + 1, sb_ref)
        softmax_pv(2 * i, 0, sa_ref)
        logits(jnp.minimum(2 * i + 2, last), sa_ref)
        softmax_pv(2 * i + 1, 1, sb_ref)
        return carry

    lax.fori_loop(0, n_tiles, att_tile, 0)

    for h in range(N_HEADS_A):
        o_ref[h * HEAD_DIM_A:(h + 1) * HEAD_DIM_A, :] = (
            acc_ref[h] / pltpu.repeat(l_ref[h], HEAD_DIM_A // SUB, axis=0)).astype(BF16)


def _dsa_prompt(qt, qit, wt, kb, vtb, kk, batch, seq, tq):
    nq = seq // tq
    ktop = min(TOPK_MAX, seq // 4)
    qcol = lambda r: pl.BlockSpec((None, r, tq), lambda b, j: (b, 0, j))
    rows = lambda w: pl.BlockSpec((seq, w), lambda b, j: (b, 0))
    return pl.pallas_call(
        functools.partial(_dsa_prompt_kernel, tq=tq, ktop=ktop),
        grid=(batch, nq),
        in_specs=[qcol(512), qcol(512), qcol(N_HEADS_IDX), rows(512),
                  pl.BlockSpec((None, 512, seq), lambda b, j: (b, 0, 0)), rows(LANES)],
        out_specs=qcol(512),
        out_shape=jax.ShapeDtypeStruct((batch, D_ATTN, seq), BF16),
        scratch_shapes=[pltpu.VMEM((nq, tq, tq), F32), pltpu.VMEM((SUB, tq), F32), pltpu.VMEM((SUB, tq), F32),
                        pltpu.VMEM((SUB, tq), F32), pltpu.VMEM((N_HEADS_A, SUB, tq), F32),
                        pltpu.VMEM((N_HEADS_A, SUB, tq), F32), pltpu.VMEM((N_HEADS_A, HEAD_DIM_A, tq), F32),
                        pltpu.VMEM((N_HEADS_A, tq // 2, tq), F32), pltpu.VMEM((N_HEADS_A, tq // 2, tq), F32)],
        compiler_params=_params("parallel", "arbitrary"),
        name="dsa_prompt",
    )(qt, qit, wt, kb, vtb, kk)


def _sample_scores_kernel(pt_ref, a_ref, w_ref, kin_ref, *rest, pages):
    page_refs = rest[:pages]
    sp_ref, sn_ref = rest[pages], rest[pages + 1]
    t_new = sn_ref.shape[0]
    a = a_ref[...]
    w = w_ref[...]

    def head_sum(s):
        r = jnp.maximum(s, 0.0) * w
        out = r[0:t_new]
        for h in range(1, N_HEADS_IDX):
            out = out + r[h * t_new:(h + 1) * t_new]
        return out

    for i in range(pages):
        ikt = page_refs[i][...].astype(BF16)
        sp_ref[:, i * PAGE_SIZE:(i + 1) * PAGE_SIZE] = head_sum(_dot(a, ikt))

    @pl.when(pl.program_id(1) == 0)
    def _():
        sn = head_sum(_dot_nt(a, kin_ref[...]))
        ti = lax.broadcasted_iota(jnp.int32, sn.shape, 0)
        si = lax.broadcasted_iota(jnp.int32, sn.shape, 1)
        sn_ref[...] = jnp.where(si <= ti, sn, -jnp.inf)


def _sample_scores(layer, page_table, a_rows, w_rows, ki_new_pad, cache_idx_kt, n_pool, pages):
    batch, n_pages = page_table.shape
    ht = a_rows.shape[1]
    t_new = ht // N_HEADS_IDX
    page_spec = lambda i: pl.BlockSpec(
        (None, D_IDX, PAGE_SIZE), lambda b, s, pt: (layer * n_pool + pt[b, s * pages + i], 0, 0))
    grid_spec = pltpu.PrefetchScalarGridSpec(
        num_scalar_prefetch=1,
        grid=(batch, n_pages // pages),
        in_specs=[pl.BlockSpec((None, ht, D_IDX), lambda b, s, pt: (b, 0, 0)),
                  pl.BlockSpec((None, ht, LANES), lambda b, s, pt: (b, 0, 0)),
                  pl.BlockSpec((None, LANES, D_IDX), lambda b, s, pt: (b, 0, 0))]
        + [page_spec(i) for i in range(pages)],
        out_specs=[pl.BlockSpec((None, t_new, pages * PAGE_SIZE), lambda b, s, pt: (b, 0, s)),
                   pl.BlockSpec((None, t_new, LANES), lambda b, s, pt: (b, 0, 0))],
    )
    return pl.pallas_call(
        functools.partial(_sample_scores_kernel, pages=pages),
        grid_spec=grid_spec,
        out_shape=[jax.ShapeDtypeStruct((batch, t_new, n_pages * PAGE_SIZE), F32),
                   jax.ShapeDtypeStruct((batch, t_new, LANES), F32)],
        compiler_params=_params("parallel", "arbitrary"),
        name="sample_scores",
    )(page_table, a_rows, w_rows, ki_new_pad, *([cache_idx_kt] * pages))


def _sample_select_kernel(sp_ref, sn_ref, selp_ref, seln_ref, sc_ref, lo_ref, hi_ref, flo_ref, *, width, ktop):
    rows, past = sp_ref.shape
    n_tiles = (past + LANES) // width
    for c in range(n_tiles):
        lo, hi = c * width, (c + 1) * width
        if hi <= past:
            sc_ref[c] = sp_ref[:, lo:hi]
        elif lo == past:
            sc_ref[c] = sn_ref[...]
        else:
            sc_ref[c] = jnp.concatenate([sp_ref[:, lo:past], sn_ref[...]], axis=1)

    def get_tile(c):
        return sc_ref[c]

    def set_tile(c, val):
        sc_ref[c] = val

    thr = _topk_threshold(get_tile, set_tile, n_tiles, rows, width, ktop, lo_ref, hi_ref, flo_ref)
    for c in range(n_tiles):
        lo, hi = c * width, (c + 1) * width
        sel = jnp.concatenate([jnp.where(ch >= thr, 1.0, 0.0) for ch in _chunks(sc_ref[c])], axis=1)
        if hi <= past:
            selp_ref[:, lo:hi] = sel
        elif lo == past:
            seln_ref[...] = sel
        else:
            selp_ref[:, lo:past] = sel[:, 0:past - lo]
            seln_ref[...] = sel[:, past - lo:]


def _sample_select(sp, sn, rows, width, ktop):
    n, past = sp.shape
    spec = lambda w: pl.BlockSpec((rows, w), lambda i: (i, 0))
    return pl.pallas_call(
        functools.partial(_sample_select_kernel, width=width, ktop=ktop),
        grid=(n // rows,),
        in_specs=[spec(past), spec(LANES)],
        out_specs=[spec(past), spec(LANES)],
        out_shape=[jax.ShapeDtypeStruct((n, past), F32), jax.ShapeDtypeStruct((n, LANES), F32)],
        scratch_shapes=[pltpu.VMEM(((past + LANES) // width, rows, width), F32), pltpu.VMEM((rows, LANES), F32),
                        pltpu.VMEM((rows, LANES), F32), pltpu.VMEM((rows, LANES), F32)],
        compiler_params=_params("parallel"),
        name="sample_select",
    )(sp, sn)


def _sample_attn_kernel(pt_ref, q_ref, selp_ref, seln_ref, kn_ref, vn_ref, *rest, pages):
    kt_refs = rest[:pages]
    vt_refs = rest[pages:2 * pages]
    o_ref, m_ref, l_ref, acc_ref = rest[2 * pages:]
    step = pl.program_id(1)
    t_new = o_ref.shape[0]
    q = q_ref[...]

    def tile_rows(sel):
        return jnp.concatenate([sel] * N_HEADS_A, axis=0)

    def update(s, pv):
        m_old = m_ref[...]
        m_new = jnp.maximum(m_old, jnp.max(s, axis=1, keepdims=True))
        alpha = jnp.exp2(m_old - m_new)
        p = jnp.exp2(s - m_new)
        l_ref[...] = alpha * l_ref[...] + jnp.sum(p, axis=1, keepdims=True)
        acc_ref[...] = alpha * acc_ref[...] + pv(p.astype(BF16))
        m_ref[...] = m_new

    @pl.when(step == 0)
    def _():
        m_ref[...] = jnp.full(m_ref.shape, NEG, F32)
        l_ref[...] = jnp.zeros(l_ref.shape, F32)
        acc_ref[...] = jnp.zeros(acc_ref.shape, F32)
        s = jnp.where(tile_rows(seln_ref[...]) > 0.5, _dot_nt(q, kn_ref[...]), NEG)
        update(s, lambda p: _dot(p, vn_ref[...]))

    s = jnp.concatenate([_dot(q, kt_refs[i][...].astype(BF16)) for i in range(pages)], axis=1)
    s = jnp.where(tile_rows(selp_ref[...]) > 0.5, s, NEG)

    def pv(p):
        out = _dot_nt(p[:, 0:PAGE_SIZE], vt_refs[0][...].astype(BF16))
        for i in range(1, pages):
            out = out + _dot_nt(p[:, i * PAGE_SIZE:(i + 1) * PAGE_SIZE], vt_refs[i][...].astype(BF16))
        return out

    update(s, pv)

    @pl.when(step == pl.num_programs(1) - 1)
    def _():
        res = acc_ref[...] / l_ref[...]
        lane_head = lax.broadcasted_iota(jnp.int32, (1, D_ATTN), 1) // HEAD_DIM_A
        out = jnp.zeros((t_new, D_ATTN), F32)
        for h in range(N_HEADS_A):
            out = out + jnp.where(lane_head == h, res[h * t_new:(h + 1) * t_new], 0.0)
        o_ref[...] = out.astype(BF16)


def _sample_attn(layer, page_table, q_rows, selp, seln, kn_pad, vn_pad, cache_kt, cache_vt, n_pool, pages):
    batch, n_pages = page_table.shape
    ht = q_rows.shape[1]
    t_new = ht // N_HEADS_A
    page_spec = lambda i: pl.BlockSpec(
        (None, D_ATTN, PAGE_SIZE), lambda b, s, pt: (layer * n_pool + pt[b, s * pages + i], 0, 0))
    per_b = lambda r, w: pl.BlockSpec((None, r, w), lambda b, s, pt: (b, 0, 0))
    grid_spec = pltpu.PrefetchScalarGridSpec(
        num_scalar_prefetch=1,
        grid=(batch, n_pages // pages),
        in_specs=[per_b(ht, D_ATTN),
                  pl.BlockSpec((None, t_new, pages * PAGE_SIZE), lambda b, s, pt: (b, 0, s)),
                  per_b(t_new, LANES), per_b(LANES, D_ATTN), per_b(LANES, D_ATTN)]
        + [page_spec(i) for i in range(pages)] * 2,
        out_specs=per_b(t_new, D_ATTN),
        scratch_shapes=[pltpu.VMEM((ht, 1), F32), pltpu.VMEM((ht, 1), F32), pltpu.VMEM((ht, D_ATTN), F32)],
    )
    return pl.pallas_call(
        functools.partial(_sample_attn_kernel, pages=pages),
        grid_spec=grid_spec,
        out_shape=jax.ShapeDtypeStruct((batch, t_new, D_ATTN), BF16),
        compiler_params=_params("parallel", "arbitrary"),
        name="sample_attn",
    )(page_table, q_rows, selp, seln, kn_pad, vn_pad, *([cache_kt] * pages), *([cache_vt] * pages))


def _merge_kernel(x_ref, a_ref, g_ref, ga_ref, gb_ref, wa_ref, wb_ref, wo_ref, n2_ref, x1_ref, h2_ref, *, a_key_major):
    a_proj = _dot_tn(a_ref[...], wa_ref[...]) if a_key_major else _dot(a_ref[...], wa_ref[...])
    mixed = (jax.nn.sigmoid(ga_ref[...]) * a_proj
             + jax.nn.sigmoid(gb_ref[...]) * _dot(g_ref[...], wb_ref[...]))
    x1 = x_ref[...] + _dot(mixed.astype(BF16), wo_ref[...])
    x1_ref[...] = x1
    h2_ref[...] = (x1 * lax.rsqrt(jnp.mean(x1 * x1, axis=-1, keepdims=True) + EPS) * n2_ref[...]).astype(BF16)


def _merge(x2d, a_o, g_o, ga, gb, w_a, w_b, w_o, norm2, batch, tm, a_key_major):
    n, d = x2d.shape
    nt = n // batch // tm
    row = lambda w: pl.BlockSpec((tm, w), lambda b, i: (b * nt + i, 0))
    a_spec = pl.BlockSpec((None, D_ATTN, tm), lambda b, i: (b, 0, i)) if a_key_major else row(D_ATTN)
    weights = (w_a.astype(BF16), w_b.astype(BF16), w_o.astype(BF16))
    return pl.pallas_call(
        functools.partial(_merge_kernel, a_key_major=a_key_major),
        grid=(batch, nt),
        in_specs=[row(d), a_spec, row(D_GLA_V), row(d), row(d)]
        + [_const_spec(w.shape) for w in weights] + [_const_spec((1, d))],
        out_specs=[row(d), row(d)],
        out_shape=[jax.ShapeDtypeStruct((n, d), F32), jax.ShapeDtypeStruct((n, d), BF16)],
        compiler_params=_params("parallel", "parallel"),
        name="merge",
    )(x2d, a_o, g_o, ga, gb, *weights, norm2.reshape(1, d))


def _ffn_kernel(x1_ref, h2_ref, wg_ref, wu_ref, wd_ref, nf_ref, y_ref, *, n_split, final_norm):
    h2 = h2_ref[...]
    y = x1_ref[...]
    fc = wg_ref.shape[1] // n_split
    for i in range(n_split):
        g = _dot(h2, wg_ref[:, i * fc:(i + 1) * fc])
        u = _dot(h2, wu_ref[:, i * fc:(i + 1) * fc])
        act = (g * jax.nn.sigmoid(g) * u).astype(BF16)
        y = y + _dot(act, wd_ref[i * fc:(i + 1) * fc, :])
    if final_norm:
        y = y * lax.rsqrt(jnp.mean(y * y, axis=-1, keepdims=True) + EPS) * nf_ref[...]
    y_ref[...] = y


def _ffn(x1, h2, w_g, w_u, w_d, norm_final, tm, final_norm):
    n, d = x1.shape
    d_ff = w_g.shape[1]
    n_split = 2 if d_ff % (2 * LANES) == 0 else 1
    row = lambda: pl.BlockSpec((tm, d), lambda i: (i, 0))
    weights = (w_g.astype(BF16), w_u.astype(BF16), w_d.astype(BF16))
    return pl.pallas_call(
        functools.partial(_ffn_kernel, n_split=n_split, final_norm=final_norm),
        grid=(n // tm,),
        in_specs=[row(), row()] + [_const_spec(w.shape) for w in weights] + [_const_spec((1, d))],
        out_specs=row(),
        out_shape=jax.ShapeDtypeStruct((n, d), F32),
        compiler_params=_params("parallel"),
        name="ffn",
    )(x1, h2, *weights, norm_final.reshape(1, d))


def _pick(n, prefs):
    for p in prefs:
        if n % p == 0:
            return p
    return n


def kernel(x_prompt, x_sample, cache_k, cache_v, cache_idx_k, state_gla, page_table, norm1, w_in, w_gate_up,
           b_gate, g_gla_norm, w_branch_a, w_branch_b, w_out, norm2, w_ffn_gate, w_ffn_up, w_ffn_down, norm_final):
    depth = w_in.shape[0]
    bp, seq, d = x_prompt.shape
    bs, t_new, _ = x_sample.shape
    n_pool = cache_k.shape[1]
    n_pages = page_table.shape[1]
    past = n_pages * PAGE_SIZE
    xp = x_prompt.reshape(bp * seq, d)
    xs = x_sample.reshape(bs * t_new, d)
    cache_kt = cache_k.transpose(0, 1, 3, 4, 2).reshape(depth * n_pool, D_ATTN, PAGE_SIZE)
    cache_vt = cache_v.transpose(0, 1, 3, 4, 2).reshape(depth * n_pool, D_ATTN, PAGE_SIZE)
    cache_ikt = cache_idx_k.transpose(0, 1, 3, 2).reshape(depth * n_pool, D_IDX, PAGE_SIZE)
    outs = [[] for _ in range(8)]
    tq = _pick(seq, (256, 128))
    for l in range(depth):
        last = l == depth - 1
        (qkg, vg, la, r, ga, gb, qt, qit, wt, kb, kk, kt, vt, vtb, kit) = _inproj(
            xp, norm1[l], w_in[l], w_gate_up[l], b_gate[l], bp, _pick(seq, (256, 128)), True)
        a_o = _dsa_prompt(qt, qit, wt, kb, vtb, kk, bp, seq, tq)
        chunk = GLA_CHUNK if seq % GLA_CHUNK == 0 else seq
        g_o, st = _gla(qkg, vg, la, r, g_gla_norm[l], jnp.zeros((bp, DV_G, D_GLA_K), F32), bp, seq, chunk,
                       _pick(seq, (512, 256, 128, 64)), _pick(bp, (2, 1)))
        x1, h2 = _merge(xp, a_o, g_o, ga, gb, w_branch_a[l], w_branch_b[l], w_out[l], norm2[l], bp,
                        _pick(seq, (512, 256, 128)), True)
        xp = _ffn(x1, h2, w_ffn_gate[l], w_ffn_up[l], w_ffn_down[l], norm_final, _pick(bp * seq, (256, 128)), last)
        outs[0].append(kt.reshape(bp, N_HEADS_A, HEAD_DIM_A, seq).transpose(0, 3, 1, 2))
        outs[1].append(vt.reshape(bp, N_HEADS_A, HEAD_DIM_A, seq).transpose(0, 3, 1, 2))
        outs[2].append(kit.transpose(0, 2, 1))
        outs[3].append(_state_from_t(st))

        (qkg, vg, la, r, ga, gb, q, qi, kw, k, v, kb, vb, kk) = _inproj(
            xs, norm1[l], w_in[l], w_gate_up[l], b_gate[l], 1, _pick(bs * t_new, (256, 128)), False)
        a_rows = qi.reshape(bs, t_new, N_HEADS_IDX, D_IDX).transpose(0, 2, 1, 3).reshape(bs, N_HEADS_IDX * t_new, D_IDX)
        w_rows = (kw[:, D_IDX:D_IDX + N_HEADS_IDX] * (N_HEADS_IDX ** -0.5 * D_IDX ** -0.5)).reshape(
            bs, t_new, N_HEADS_IDX).transpose(0, 2, 1).reshape(bs, N_HEADS_IDX * t_new, 1)
        w_rows = jnp.broadcast_to(w_rows, (bs, N_HEADS_IDX * t_new, LANES))
        pad_rows = lambda a: jnp.pad(a.reshape(bs, t_new, a.shape[-1]), ((0, 0), (0, LANES - t_new), (0, 0)))
        sp, sn = _sample_scores(l, page_table, a_rows, w_rows, pad_rows(kk[:, 0:D_IDX]), cache_ikt, n_pool,
                                _pick(n_pages, (32, 16, 8, 4, 2, 1)))
        ktop = min(TOPK_MAX, (past + t_new) // 4)
        width = 5 * LANES if (past + LANES) % (5 * LANES) == 0 else LANES
        rows = _pick(bs * t_new, (128, 64, 32, 16, 8))
        selp, seln = _sample_select(sp.reshape(bs * t_new, past), sn.reshape(bs * t_new, LANES), rows, width, ktop)
        head_of_lane = jnp.arange(D_ATTN) // HEAD_DIM_A
        q_rows = jnp.where(head_of_lane[None, None, None, :] == jnp.arange(N_HEADS_A)[None, :, None, None],
                           q.reshape(bs, 1, t_new, D_ATTN), jnp.zeros((), BF16)).reshape(bs, N_HEADS_A * t_new, D_ATTN)
        a_o = _sample_attn(l, page_table, q_rows, selp.reshape(bs, t_new, past), seln.reshape(bs, t_new, LANES),
                           pad_rows(kb), pad_rows(vb), cache_kt, cache_vt, n_pool, _pick(n_pages, (16, 8, 4, 2, 1)))
        gchunk = 16
        pad_g = lambda a: jnp.pad(a.reshape(bs, t_new, a.shape[-1]), ((0, 0), (0, gchunk - t_new), (0, 0))).reshape(
            bs * gchunk, a.shape[-1])
        g_o, st = _gla(pad_g(qkg), pad_g(vg), pad_g(la), pad_g(r), g_gla_norm[l], _state_to_t(state_gla[l].astype(F32)),
                       bs, gchunk, gchunk, gchunk, _pick(bs, (8, 4, 2, 1)))
        g_o = g_o.reshape(bs, gchunk, D_GLA_V)[:, 0:t_new].reshape(bs * t_new, D_GLA_V)
        x1, h2 = _merge(xs, a_o.reshape(bs * t_new, D_ATTN), g_o, ga, gb, w_branch_a[l], w_branch_b[l], w_out[l],
                        norm2[l], 1, _pick(bs * t_new, (512, 256, 128)), False)
        xs = _ffn(x1, h2, w_ffn_gate[l], w_ffn_up[l], w_ffn_down[l], norm_final, _pick(bs * t_new, (256, 128)), last)
        outs[4].append(k.reshape(bs, t_new, N_HEADS_A, HEAD_DIM_A))
        outs[5].append(v.reshape(bs, t_new, N_HEADS_A, HEAD_DIM_A))
        outs[6].append(kw[:, 0:D_IDX].reshape(bs, t_new, D_IDX))
        outs[7].append(_state_from_t(st))
    return (xp.reshape(bp, seq, d), xs.reshape(bs, t_new, d)) + tuple(jnp.stack(o) for o in outs)
```

```python
import functools

import jax
import jax.numpy as jnp
from jax import lax
from jax.experimental import pallas as pl
from jax.experimental.pallas import tpu as pltpu

F32 = jnp.float32
BF16 = jnp.bfloat16

D_ATTN = 512
N_HEADS_A = 8
HEAD_DIM_A = 64
N_HEADS_IDX = 8
D_IDX = 64
TOPK_MAX = 256
N_HEADS_G = 4
DK_G = 64
DV_G = 128
D_GLA_K = N_HEADS_G * DK_G
D_GLA_V = N_HEADS_G * DV_G
GATE_RANK = 16
GATE_TAU = 16.0
GLA_CHUNK = 64
EPS = 1e-6
PAGE_SIZE = 128

LANES = 128
NEG = -1e30
LOWEST = -3.0e38
MAX_BISECT = 64
Q_SCALE = HEAD_DIM_A ** -0.5 * 1.4426950408889634
VMEM_LIMIT = 56 * 1024 * 1024


def _dot(a, b):
    return jnp.dot(a, b, preferred_element_type=F32)


def _dot_nt(a, b):
    return lax.dot_general(a, b, (((1,), (1,)), ((), ())), preferred_element_type=F32)


def _dot_tn(a, b):
    return lax.dot_general(a, b, (((0,), (0,)), ((), ())), preferred_element_type=F32)


def _params(*sem):
    return pltpu.CompilerParams(dimension_semantics=sem, vmem_limit_bytes=VMEM_LIMIT)


def _const_spec(shape):
    n = len(shape)
    return pl.BlockSpec(shape, lambda *_: (0,) * n)


def _chunks(x):
    return [x[:, i * LANES:(i + 1) * LANES] for i in range(x.shape[1] // LANES)]


def _rep(col, rows):
    return jnp.broadcast_to(col, (rows, LANES))


def _inproj_kernel(x_ref, g_ref, wc_ref, wd_ref, we_ref, wgu_ref, bg_ref, *refs, key_major):
    x = x_ref[...]
    h = (x * lax.rsqrt(jnp.mean(x * x, axis=-1, keepdims=True) + EPS) * g_ref[...]).astype(BF16)
    if key_major:
        (wqt_ref, wkvt_ref, wkk_ref, wkit_ref, wwit_ref,
         qkg_ref, vg_ref, la_ref, r_ref, ga_ref, gb_ref,
         qt_ref, qit_ref, wt_ref, kb_ref, kk_ref, kt_ref, vt_ref, vtb_ref, kit_ref) = refs
        qq = _dot_nt(wqt_ref[...], h)
        qt_ref[...] = (qq[0:512] * Q_SCALE).astype(BF16)
        qit_ref[...] = qq[512:1024].astype(BF16)
        kv = _dot_nt(wkvt_ref[...], h)
        kt_ref[...] = kv[0:512]
        vt_ref[...] = kv[512:1024]
        vtb_ref[...] = kv[512:1024].astype(BF16)
        kb_ref[...] = kv[0:512].T.astype(BF16)
        kk_ref[...] = _dot(h, wkk_ref[...]).astype(BF16)
        kit_ref[...] = _dot_nt(wkit_ref[...], h)
        wt_ref[...] = _dot_nt(wwit_ref[...], h)[0:N_HEADS_IDX] * (N_HEADS_IDX ** -0.5 * D_IDX ** -0.5)
    else:
        (wq_ref, wkv_ref, wkk_ref, wkw_ref,
         qkg_ref, vg_ref, la_ref, r_ref, ga_ref, gb_ref,
         q_ref, qi_ref, kw_ref, k_ref, v_ref, kb_ref, vb_ref, kk_ref) = refs
        a = _dot(h, wq_ref[...])
        q_ref[...] = (a[:, 0:512] * Q_SCALE).astype(BF16)
        qi_ref[...] = a[:, 512:1024].astype(BF16)
        kw_ref[...] = _dot(h, wkw_ref[...])
        kv = _dot(h, wkv_ref[...])
        k_ref[...] = kv[:, 0:512]
        v_ref[...] = kv[:, 512:1024]
        kb_ref[...] = kv[:, 0:512].astype(BF16)
        vb_ref[...] = kv[:, 512:1024].astype(BF16)
        kk_ref[...] = _dot(h, wkk_ref[...]).astype(BF16)
    c = _dot(h, wc_ref[...])
    qkg_ref[...] = c[:, 0:512]
    vg_ref[...] = c[:, 512:1024]
    glr = _dot(h, wd_ref[...]).astype(BF16)
    z = _dot(glr, wgu_ref[...]) + bg_ref[...]
    la_ref[...] = (jnp.minimum(z, 0.0) - jnp.log1p(jnp.exp(-jnp.abs(z)))) * (1.0 / GATE_TAU)
    e = _dot(h, we_ref[...])
    r_ref[...] = e[:, 0:512]
    ga_ref[...] = e[:, 512:1536]
    gb_ref[...] = e[:, 1536:2560]


def _inproj(x2d, norm_g, w_in, w_gate_up, b_gate, batch, tm, key_major):
    n, d = x2d.shape
    t_len = n // batch
    nt = t_len // tm
    wb = w_in.astype(BF16)
    wq = jnp.concatenate([wb[:, 0:512], wb[:, 1536:2048]], axis=1)
    wkv = wb[:, 512:1536]
    wki = wb[:, 2048:2112]
    wwi = wb[:, 2112:2120]
    wkk = jnp.concatenate([wki, wki], axis=1)
    wc = wb[:, 2120:3144]
    wd = jnp.concatenate([wb[:, 3144:3160], jnp.zeros((d, LANES - GATE_RANK), BF16)], axis=1)
    we = wb[:, 3160:5720]
    wgu = jnp.concatenate([w_gate_up.astype(BF16), jnp.zeros((LANES - GATE_RANK, D_GLA_K), BF16)], axis=0)
    row = lambda w: pl.BlockSpec((tm, w), lambda b, i: (b * nt + i, 0))
    col = lambda r: pl.BlockSpec((None, r, tm), lambda b, i: (b, 0, i))
    rows_out = lambda outs: ([row(w) for w, _ in outs], [jax.ShapeDtypeStruct((n, w), dt) for w, dt in outs])
    cols_out = lambda outs: ([col(r) for r, _ in outs],
                             [jax.ShapeDtypeStruct((batch, r, t_len), dt) for r, dt in outs])
    out_specs, out_shape = rows_out([(512, F32), (512, F32), (256, F32), (512, F32), (1024, F32), (1024, F32)])
    if key_major:
        wwit = jnp.concatenate([wwi.T, jnp.zeros((16 - N_HEADS_IDX, d), BF16)], axis=0)
        flavor_w = (wq.T, wkv.T, wkk, wki.T, wwit)
        s1, h1 = cols_out([(512, BF16), (512, BF16), (N_HEADS_IDX, F32)])
        s2, h2 = rows_out([(512, BF16), (LANES, BF16)])
        s3, h3 = cols_out([(512, F32), (512, F32), (512, BF16), (D_IDX, F32)])
        out_specs, out_shape = out_specs + s1 + s2 + s3, out_shape + h1 + h2 + h3
    else:
        wkw = jnp.concatenate([wki, wwi, jnp.zeros((d, LANES - 72), BF16)], axis=1)
        flavor_w = (wq, wkv, wkk, wkw)
        s1, h1 = rows_out([(512, BF16), (512, BF16), (LANES, F32), (512, F32), (512, F32), (512, BF16),
                           (512, BF16), (LANES, BF16)])
        out_specs, out_shape = out_specs + s1, out_shape + h1
    weights = (wc, wd, we, wgu)
    return pl.pallas_call(
        functools.partial(_inproj_kernel, key_major=key_major),
        grid=(batch, nt),
        in_specs=[row(d), _const_spec((1, d))] + [_const_spec(w.shape) for w in weights]
        + [_const_spec((1, D_GLA_K))] + [_const_spec(w.shape) for w in flavor_w],
        out_specs=out_specs,
        out_shape=out_shape,
        compiler_params=_params("parallel", "parallel"),
        name="inproj",
    )(x2d, norm_g.reshape(1, d), *weights, b_gate.reshape(1, D_GLA_K), *flavor_w)


def _gla_kernel(qk_ref, v_ref, la_ref, r_ref, gn_ref, s0_ref, go_ref, sfin_ref, st_ref, *, chunk, n_chunks):
    step = pl.program_id(1)

    @pl.when(step == 0)
    def _():
        st_ref[...] = s0_ref[...]

    ri = lax.broadcasted_iota(jnp.int32, (chunk, chunk), 0)
    ci = lax.broadcasted_iota(jnp.int32, (chunk, chunk), 1)
    tril = ci <= ri
    tri = jnp.where(tril, 1.0, 0.0).astype(F32)
    lane_head = lax.broadcasted_iota(jnp.int32, (1, D_GLA_K), 1) // DK_G
    mid = chunk // 2 - 1
    gn = gn_ref[...]

    def body(c, carry):
        for g in range(st_ref.shape[0]):
            one_chunk(c, g)
        return carry

    def one_chunk(c, g):
        r0 = pl.multiple_of(c * chunk, chunk)
        qk = qk_ref[g, pl.ds(r0, chunk), :]
        q = qk[:, 0:D_GLA_K] * DK_G ** -0.5
        k = qk[:, D_GLA_K:2 * D_GLA_K]
        la = la_ref[g, pl.ds(r0, chunk), :]
        b = jnp.dot(tri, la, precision=lax.Precision.HIGHEST, preferred_element_type=F32)
        b_mid = b[mid:mid + 1, :]
        b_end = b[chunk - 1:chunk, :]
        qe = q * jnp.exp(b)
        qt = q * jnp.exp(b - b_mid)
        ktb = (k * jnp.exp(b_mid - b)).astype(BF16)
        kdb = (k * jnp.exp(b_end - b)).astype(BF16)
        st = st_ref[g]
        stb = st.astype(BF16)
        new_st = st * jnp.exp(b_end)
        for h in range(N_HEADS_G):
            hm = lane_head == h
            qt_h = jnp.where(hm, qt, 0.0).astype(BF16)
            qe_h = jnp.where(hm, qe, 0.0).astype(BF16)
            att = jnp.where(tril, _dot_nt(qt_h, ktb), 0.0)
            vb = v_ref[g, pl.ds(r0, chunk), h * DV_G:(h + 1) * DV_G].astype(BF16)
            o = _dot_nt(qe_h, stb) + _dot(att.astype(BF16), vb)
            new_st = new_st + jnp.where(hm, _dot_tn(vb, kdb), 0.0)
            on = o * lax.rsqrt(jnp.mean(o * o, axis=-1, keepdims=True) + EPS) * gn
            rr = r_ref[g, pl.ds(r0, chunk), h * DV_G:(h + 1) * DV_G]
            go_ref[g, pl.ds(r0, chunk), h * DV_G:(h + 1) * DV_G] = (on * (rr * jax.nn.sigmoid(rr))).astype(BF16)
        st_ref[g] = new_st

    lax.fori_loop(0, n_chunks, body, 0)

    @pl.when(step == pl.num_programs(1) - 1)
    def _():
        sfin_ref[...] = st_ref[...]


def _gla(qkg, vg, la, r, g_norm, s0_t, batch, t_len, chunk, rows_per_step, group):
    n_steps = t_len // rows_per_step
    row = lambda w: pl.BlockSpec((group, rows_per_step, w), lambda b, s: (b, s, 0))
    st_spec = pl.BlockSpec((group, DV_G, D_GLA_K), lambda b, s: (b, 0, 0))
    seqs = lambda a: a.reshape(batch, t_len, a.shape[-1])
    g_o, st = pl.pallas_call(
        functools.partial(_gla_kernel, chunk=chunk, n_chunks=rows_per_step // chunk),
        grid=(batch // group, n_steps),
        in_specs=[row(512), row(512), row(256), row(512), _const_spec((1, DV_G)), st_spec],
        out_specs=[row(512), st_spec],
        out_shape=[jax.ShapeDtypeStruct((batch, t_len, D_GLA_V), BF16),
                   jax.ShapeDtypeStruct((batch, DV_G, D_GLA_K), F32)],
        scratch_shapes=[pltpu.VMEM((group, DV_G, D_GLA_K), F32)],
        compiler_params=_params("parallel", "arbitrary"),
        name="gla",
    )(seqs(qkg), seqs(vg), seqs(la), seqs(r), g_norm.reshape(1, DV_G), s0_t)
    return g_o.reshape(batch * t_len, D_GLA_V), st


def _state_to_t(s):
    b = s.shape[0]
    return s.transpose(0, 3, 1, 2).reshape(b, DV_G, D_GLA_K)


def _state_from_t(st):
    b = st.shape[0]
    return st.reshape(b, DV_G, N_HEADS_G, DK_G).transpose(0, 2, 3, 1)


def _topk_threshold(get_tile, set_tile, n_tiles, rows, width, ktop, lo_ref, hi_ref, flo_ref):
    kf = float(ktop)
    one = lambda m: jnp.where(m, 1.0, 0.0)
    zeros = jnp.zeros((rows, LANES), F32)
    rsum = lambda a: _rep(jnp.sum(a, axis=1, keepdims=True), rows)

    def acc_chunks(blk, acc, fn, op):
        for ch in _chunks(blk):
            acc = op(acc, fn(ch))
        return acc

    def stats(c, st):
        mx, mn, nv, c0, c0e = st
        blk = get_tile(c)
        mx = acc_chunks(blk, mx, lambda ch: ch, jnp.maximum)
        mn = acc_chunks(blk, mn, lambda ch: jnp.where(ch > -jnp.inf, ch, jnp.inf), jnp.minimum)
        nv = acc_chunks(blk, nv, lambda ch: one(ch > -jnp.inf), jnp.add)
        c0 = acc_chunks(blk, c0, lambda ch: one(ch > 0.0), jnp.add)
        c0e = acc_chunks(blk, c0e, lambda ch: one(ch >= 0.0), jnp.add)
        return mx, mn, nv, c0, c0e

    mx, mn, nv, c0, c0e = lax.fori_loop(
        0, n_tiles, stats,
        (jnp.full((rows, LANES), -jnp.inf, F32), jnp.full((rows, LANES), jnp.inf, F32), zeros, zeros, zeros))
    rowmax = _rep(jnp.max(mx, axis=1, keepdims=True), rows)
    rowmin = _rep(jnp.min(mn, axis=1, keepdims=True), rows)
    nvalid, cnt0, cnt0e = rsum(nv), rsum(c0), rsum(c0e)

    def count_gt(thr):
        def body(c, acc):
            return acc_chunks(get_tile(c), acc, lambda ch: one(ch > thr), jnp.add)
        return rsum(lax.fori_loop(0, n_tiles, body, zeros))

    small = nvalid <= kf
    ztie = jnp.logical_and(jnp.logical_and(cnt0 < kf, cnt0e >= kf), jnp.logical_not(small))
    frozen = jnp.logical_or(small, ztie)
    pos = cnt0 >= kf
    lo_ref[...] = jnp.where(frozen, 0.0, jnp.where(pos, 0.0, jnp.minimum(rowmin - 1.0, rowmin * 2.0)))
    hi_ref[...] = jnp.where(frozen, 0.0, jnp.where(pos, rowmax, 0.0))
    flo_ref[...] = jnp.where(frozen, kf, jnp.where(pos, cnt0, nvalid))

    def cond(st):
        it, pending = st
        return jnp.logical_and(it < MAX_BISECT, pending > 0)

    def bisect(st):
        it, _ = st
        lo, hi = lo_ref[...], hi_ref[...]
        mid = 0.5 * (lo + hi)
        c = count_gt(mid)
        ge = c >= kf
        flo = jnp.where(ge, c, flo_ref[...])
        lo_ref[...] = jnp.where(ge, mid, lo)
        hi_ref[...] = jnp.where(ge, hi, mid)
        flo_ref[...] = flo
        return it + 1, jnp.max(one(flo != kf)).astype(jnp.int32)

    pending0 = jnp.max(one(flo_ref[...] != kf)).astype(jnp.int32)
    lax.while_loop(cond, bisect, (jnp.int32(0), pending0))

    lo = lo_ref[...]

    def tmin(c, acc):
        return acc_chunks(get_tile(c), acc, lambda ch: jnp.where(ch > lo, ch, jnp.inf), jnp.minimum)

    thr = _rep(jnp.min(lax.fori_loop(0, n_tiles, tmin, jnp.full((rows, LANES), jnp.inf, F32)),
                       axis=1, keepdims=True), rows)
    thr = jnp.where(ztie, 0.0, thr)
    thr = jnp.where(small, LOWEST, thr)

    def gteq(c, st):
        g, e = st
        blk = get_tile(c)
        return (acc_chunks(blk, g, lambda ch: one(ch > thr), jnp.add),
                acc_chunks(blk, e, lambda ch: one(ch == thr), jnp.add))

    g, e = lax.fori_loop(0, n_tiles, gteq, (zeros, zeros))
    need = kf - rsum(g)
    any_tie = jnp.max(one(rsum(e) > need)) > 0.0

    @pl.when(any_tie)
    def _():
        ri = lax.broadcasted_iota(jnp.int32, (width, width), 0)
        ci = lax.broadcasted_iota(jnp.int32, (width, width), 1)
        upper = jnp.where(ri <= ci, 1.0, 0.0).astype(BF16)
        thr_c, need_c = thr[:, 0:1], need[:, 0:1]

        def drop(c, seen):
            blk = get_tile(c)
            eq = blk == thr_c
            eqf = one(eq)
            rank = _dot(eqf.astype(BF16), upper) + seen - 1.0
            set_tile(c, jnp.where(jnp.logical_and(eq, rank >= need_c), -jnp.inf, blk))
            return seen + jnp.sum(eqf, axis=1, keepdims=True)

        lax.fori_loop(0, n_tiles, drop, jnp.zeros((rows, 1), F32))

    return thr


SUB = 8


def _groups(x):
    return [x[i * SUB:(i + 1) * SUB] for i in range(x.shape[0] // SUB)]


def _tree(op, xs):
    xs = list(xs)
    while len(xs) > 1:
        xs = [op(xs[i], xs[i + 1]) if i + 1 < len(xs) else xs[i] for i in range(0, len(xs), 2)]
    return xs[0]


def _col(op, x8):
    return jnp.broadcast_to(op(x8, axis=0, keepdims=True), x8.shape)


def _col_stats_init(n):
    zeros = jnp.zeros((SUB, n), F32)
    return jnp.full((SUB, n), -jnp.inf, F32), jnp.full((SUB, n), jnp.inf, F32), zeros, zeros, zeros


def _col_stats_update(st, tile):
    one = lambda m: jnp.where(m, 1.0, 0.0)
    mx, mn, nv, c0, c0e = st
    g = _groups(tile)
    mx = jnp.maximum(mx, _tree(jnp.maximum, g))
    mn = jnp.minimum(mn, _tree(jnp.minimum, [jnp.where(x > -jnp.inf, x, jnp.inf) for x in g]))
    nv = nv + _tree(jnp.add, [one(x > -jnp.inf) for x in g])
    c0 = c0 + _tree(jnp.add, [one(x > 0.0) for x in g])
    c0e = c0e + _tree(jnp.add, [one(x >= 0.0) for x in g])
    return mx, mn, nv, c0, c0e


def _topk_cut_cols(get_tile, set_tile, n_tiles, tk, n, ktop, stats, lo_ref, hi_ref, flo_ref):
    kf = float(ktop)
    one = lambda m: jnp.where(m, 1.0, 0.0)
    zeros = jnp.zeros((SUB, n), F32)
    total = lambda a: _col(jnp.sum, a)
    mx, mn, nv, c0, c0e = stats
    colmax, colmin = _col(jnp.max, mx), _col(jnp.min, mn)
    nvalid, cnt0, cnt0e = total(nv), total(c0), total(c0e)

    def count_gt(thr):
        def body(c, acc):
            return acc + _tree(jnp.add, [one(x > thr) for x in _groups(get_tile(c))])
        return total(lax.fori_loop(0, n_tiles, body, zeros))

    small = nvalid <= kf
    ztie = jnp.logical_and(jnp.logical_and(cnt0 < kf, cnt0e >= kf), jnp.logical_not(small))
    frozen = jnp.logical_or(small, ztie)
    pos = cnt0 >= kf
    lo_ref[...] = jnp.where(frozen, 0.0, jnp.where(pos, 0.0, jnp.minimum(colmin - 1.0, colmin * 2.0)))
    hi_ref[...] = jnp.where(frozen, 0.0, jnp.where(pos, colmax, 0.0))
    flo_ref[...] = jnp.where(frozen, kf, jnp.where(pos, cnt0, nvalid))

    def cond(st):
        it, pending = st
        return jnp.logical_and(it < MAX_BISECT, pending > 0)

    def bisect_until(slack):
        def bisect(st):
            it, _ = st
            lo, hi = lo_ref[...], hi_ref[...]
            mid = 0.5 * (lo + hi)
            c = count_gt(mid)
            ge = c >= kf
            flo = jnp.where(ge, c, flo_ref[...])
            lo_ref[...] = jnp.where(ge, mid, lo)
            hi_ref[...] = jnp.where(ge, hi, mid)
            flo_ref[...] = flo
            return it + 1, jnp.max(one(flo > kf + slack)).astype(jnp.int32)
        return bisect

    def next_above(lo):
        def body(c, acc):
            return jnp.minimum(
                acc, _tree(jnp.minimum, [jnp.where(x > lo, x, jnp.inf) for x in _groups(get_tile(c))]))
        return _col(jnp.min, lax.fori_loop(0, n_tiles, body, jnp.full((SUB, n), jnp.inf, F32)))

    pending0 = jnp.max(one(flo_ref[...] > kf + 1.0)).astype(jnp.int32)
    it1, _ = lax.while_loop(cond, bisect_until(1.0), (jnp.int32(0), pending0))
    lo1, flo1 = lo_ref[...], flo_ref[...]
    lo_adv = jnp.where(flo1 == kf + 1.0, next_above(lo1), lo1)
    stepped = count_gt(lo_adv) == kf
    lo_ref[...] = jnp.where(stepped, lo_adv, lo1)
    flo_ref[...] = jnp.where(stepped, kf, flo1)
    pending1 = jnp.max(one(flo_ref[...] != kf)).astype(jnp.int32)
    lax.while_loop(cond, bisect_until(0.0), (it1, pending1))

    lo = lo_ref[...]
    unresolved = jnp.logical_or(ztie, flo_ref[...] != kf)
    lo_ref[...] = jnp.where(small, -jnp.inf, lo)

    @pl.when(jnp.max(one(unresolved)) > 0.0)
    def _():
        thr = jnp.where(ztie, 0.0, next_above(lo))

        def gt(c, g):
            return g + _tree(jnp.add, [one(x > thr) for x in _groups(get_tile(c))])

        need = kf - total(lax.fori_loop(0, n_tiles, gt, zeros))
        ri = lax.broadcasted_iota(jnp.int32, (tk, tk), 0)
        ci = lax.broadcasted_iota(jnp.int32, (tk, tk), 1)
        lower = jnp.where(ci <= ri, 1.0, 0.0).astype(BF16)
        thr_r, need_r, unres_r = thr[0:1], need[0:1], unresolved[0:1]

        def drop(c, st):
            seen, below = st
            blk = get_tile(c)
            eq = jnp.logical_and(blk == thr_r, unres_r)
            eqf = one(eq)
            rank = _dot(lower, eqf.astype(BF16)) + seen - 1.0
            set_tile(c, jnp.where(jnp.logical_and(eq, rank >= need_r), -jnp.inf, blk))
            below = jnp.maximum(below, jnp.max(jnp.where(blk < thr_r, blk, -jnp.inf), axis=0, keepdims=True))
            return seen + jnp.sum(eqf, axis=0, keepdims=True), below

        _, below = lax.fori_loop(0, n_tiles, drop, (jnp.zeros((1, n), F32), jnp.full((1, n), -jnp.inf, F32)))
        lo_ref[...] = jnp.where(unresolved, jnp.broadcast_to(below, (SUB, n)), lo_ref[...])

    return lo_ref[...]


def _dsa_prompt_kernel(qt_ref, qit_ref, wt_ref, k_ref, vt_ref, kk_ref, o_ref,
                       sc_ref, lo_ref, hi_ref, flo_ref, m_ref, l_ref, acc_ref, sa_ref, sb_ref, *, tq, ktop):
    j = pl.program_id(1)
    n_tiles = j + 1
    t0 = j * tq
    qpos = t0 + lax.broadcasted_iota(jnp.int32, (1, tq), 1)
    kiota = lax.broadcasted_iota(jnp.int32, (tq, 1), 0)
    per = LANES // HEAD_DIM_A
    reps = tq // SUB
    zero_rows = jnp.zeros((LANES - D_IDX, tq), BF16)

    def tall(x8):
        return pltpu.repeat(x8, reps, axis=0)

    qit = qit_ref[...]
    wt = wt_ref[...]
    qih = [jnp.concatenate([qit[h * D_IDX:(h + 1) * D_IDX], zero_rows], axis=0) for h in range(N_HEADS_IDX)]
    wrow = [tall(jnp.broadcast_to(wt[h:h + 1], (SUB, tq))) for h in range(N_HEADS_IDX)]

    def score_tile(c, stats):
        k0 = pl.multiple_of(c * tq, tq)
        kk = kk_ref[pl.ds(k0, tq), :]
        acc = jnp.zeros((tq, tq), F32)
        for h in range(N_HEADS_IDX):
            acc = acc + jnp.maximum(_dot(kk, qih[h]), 0.0) * wrow[h]
        tile = jnp.where(k0 + kiota <= qpos, acc, -jnp.inf)
        sc_ref[c] = tile
        return _col_stats_update(stats, tile)

    stats = lax.fori_loop(0, n_tiles, score_tile, _col_stats_init(tq))

    def get_tile(c):
        return sc_ref[c]

    def set_tile(c, val):
        sc_ref[c] = val

    cut = tall(_topk_cut_cols(get_tile, set_tile, n_tiles, tq, tq, ktop, stats, lo_ref, hi_ref, flo_ref))

    qt = qt_ref[...]
    zq = jnp.zeros((HEAD_DIM_A, tq), BF16)
    qh = []
    for h in range(N_HEADS_A):
        rows = qt[h * HEAD_DIM_A:(h + 1) * HEAD_DIM_A]
        qh.append(jnp.concatenate([zq] * (h % per) + [rows] + [zq] * (per - 1 - h % per), axis=0))
    m_ref[...] = jnp.full(m_ref.shape, NEG, F32)
    l_ref[...] = jnp.zeros(l_ref.shape, F32)
    acc_ref[...] = jnp.zeros(acc_ref.shape, F32)

    ta = tq // 2
    cut_a = cut[0:ta]
    ones_rows = jnp.ones((2 * SUB, ta), BF16)
    last = 2 * n_tiles - 1

    def logits(c, buf):
        k0 = pl.multiple_of(c * ta, ta)
        for h in range(N_HEADS_A):
            buf[h] = _dot(k_ref[pl.ds(k0, ta), (h // per) * LANES:(h // per + 1) * LANES], qh[h])

    def softmax_pv(c, half, buf):
        k0 = pl.multiple_of(c * ta, ta)
        bias = jnp.where(sc_ref[c // 2, half * ta:(half + 1) * ta, :] > cut_a, 0.0, NEG)
        for h in range(N_HEADS_A):
            s = buf[h] + bias
            vh = jnp.concatenate([vt_ref[h * HEAD_DIM_A:(h + 1) * HEAD_DIM_A, pl.ds(k0, ta)], ones_rows], axis=0)
            m_old = m_ref[h]
            m_new = jnp.maximum(m_old, _col(jnp.max, _tree(jnp.maximum, _groups(s))))
            alpha = jnp.exp2(m_old - m_new)
            p = jnp.exp2(s - pltpu.repeat(m_new, ta // SUB, axis=0))
            pv = _dot(vh, p.astype(BF16))
            l_ref[h] = alpha * l_ref[h] + pv[HEAD_DIM_A:HEAD_DIM_A + SUB]
            acc_ref[h] = pltpu.repeat(alpha, HEAD_DIM_A // SUB, axis=0) * acc_ref[h] + pv[0:HEAD_DIM_A]
            m_ref[h] = m_new

    logits(0, sa_ref)

    def att_tile(i, carry):
        logits(2 * i + 1, sb_ref)
        softmax_pv(2 * i, 0, sa_ref)
        logits(jnp.minimum(2 * i + 2, last), sa_ref)
        softmax_pv(2 * i + 1, 1, sb_ref)
        return carry

    lax.fori_loop(0, n_tiles, att_tile, 0)

    for h in range(N_HEADS_A):
        o_ref[h * HEAD_DIM_A:(h + 1) * HEAD_DIM_A, :] = (
            acc_ref[h] / pltpu.repeat(l_ref[h], HEAD_DIM_A // SUB, axis=0)).astype(BF16)


def _dsa_prompt(qt, qit, wt, kb, vtb, kk, batch, seq, tq):
    nq = seq // tq
    ktop = min(TOPK_MAX, seq // 4)
    qcol = lambda r: pl.BlockSpec((None, r, tq), lambda b, j: (b, 0, j))
    rows = lambda w: pl.BlockSpec((seq, w), lambda b, j: (b, 0))
    return pl.pallas_call(
        functools.partial(_dsa_prompt_kernel, tq=tq, ktop=ktop),
        grid=(batch, nq),
        in_specs=[qcol(512), qcol(512), qcol(N_HEADS_IDX), rows(512),
                  pl.BlockSpec((None, 512, seq), lambda b, j: (b, 0, 0)), rows(LANES)],
        out_specs=qcol(512),
        out_shape=jax.ShapeDtypeStruct((batch, D_ATTN, seq), BF16),
        scratch_shapes=[pltpu.VMEM((nq, tq, tq), F32), pltpu.VMEM((SUB, tq), F32), pltpu.VMEM((SUB, tq), F32),
                        pltpu.VMEM((SUB, tq), F32), pltpu.VMEM((N_HEADS_A, SUB, tq), F32),
                        pltpu.VMEM((N_HEADS_A, SUB, tq), F32), pltpu.VMEM((N_HEADS_A, HEAD_DIM_A, tq), F32),
                        pltpu.VMEM((N_HEADS_A, tq // 2, tq), F32), pltpu.VMEM((N_HEADS_A, tq // 2, tq), F32)],
        compiler_params=_params("parallel", "arbitrary"),
        name="dsa_prompt",
    )(qt, qit, wt, kb, vtb, kk)


def _sample_scores_kernel(pt_ref, a_ref, w_ref, kin_ref, *rest, pages):
    page_refs = rest[:pages]
    sp_ref, sn_ref = rest[pages], rest[pages + 1]
    t_new = sn_ref.shape[0]
    a = a_ref[...]
    w = w_ref[...]

    def head_sum(s):
        r = jnp.maximum(s, 0.0) * w
        out = r[0:t_new]
        for h in range(1, N_HEADS_IDX):
            out = out + r[h * t_new:(h + 1) * t_new]
        return out

    for i in range(pages):
        ikt = page_refs[i][...].astype(BF16)
        sp_ref[:, i * PAGE_SIZE:(i + 1) * PAGE_SIZE] = head_sum(_dot(a, ikt))

    @pl.when(pl.program_id(1) == 0)
    def _():
        sn = head_sum(_dot_nt(a, kin_ref[...]))
        ti = lax.broadcasted_iota(jnp.int32, sn.shape, 0)
        si = lax.broadcasted_iota(jnp.int32, sn.shape, 1)
        sn_ref[...] = jnp.where(si <= ti, sn, -jnp.inf)


def _sample_scores(layer, page_table, a_rows, w_rows, ki_new_pad, cache_idx_kt, n_pool, pages):
    batch, n_pages = page_table.shape
    ht = a_rows.shape[1]
    t_new = ht // N_HEADS_IDX
    page_spec = lambda i: pl.BlockSpec(
        (None, D_IDX, PAGE_SIZE), lambda b, s, pt: (layer * n_pool + pt[b, s * pages + i], 0, 0))
    grid_spec = pltpu.PrefetchScalarGridSpec(
        num_scalar_prefetch=1,
        grid=(batch, n_pages // pages),
        in_specs=[pl.BlockSpec((None, ht, D_IDX), lambda b, s, pt: (b, 0, 0)),
                  pl.BlockSpec((None, ht, LANES), lambda b, s, pt: (b, 0, 0)),
                  pl.BlockSpec((None, LANES, D_IDX), lambda b, s, pt: (b, 0, 0))]
        + [page_spec(i) for i in range(pages)],
        out_specs=[pl.BlockSpec((None, t_new, pages * PAGE_SIZE), lambda b, s, pt: (b, 0, s)),
                   pl.BlockSpec((None, t_new, LANES), lambda b, s, pt: (b, 0, 0))],
    )
    return pl.pallas_call(
        functools.partial(_sample_scores_kernel, pages=pages),
        grid_spec=grid_spec,
        out_shape=[jax.ShapeDtypeStruct((batch, t_new, n_pages * PAGE_SIZE), F32),
                   jax.ShapeDtypeStruct((batch, t_new, LANES), F32)],
        compiler_params=_params("parallel", "arbitrary"),
        name="sample_scores",
    )(page_table, a_rows, w_rows, ki_new_pad, *([cache_idx_kt] * pages))


def _sample_select_kernel(sp_ref, sn_ref, selp_ref, seln_ref, sc_ref, lo_ref, hi_ref, flo_ref, *, width, ktop):
    rows, past = sp_ref.shape
    n_tiles = (past + LANES) // width
    for c in range(n_tiles):
        lo, hi = c * width, (c + 1) * width
        if hi <= past:
            sc_ref[c] = sp_ref[:, lo:hi]
        elif lo == past:
            sc_ref[c] = sn_ref[...]
        else:
            sc_ref[c] = jnp.concatenate([sp_ref[:, lo:past], sn_ref[...]], axis=1)

    def get_tile(c):
        return sc_ref[c]

    def set_tile(c, val):
        sc_ref[c] = val

    thr = _topk_threshold(get_tile, set_tile, n_tiles, rows, width, ktop, lo_ref, hi_ref, flo_ref)
    for c in range(n_tiles):
        lo, hi = c * width, (c + 1) * width
        sel = jnp.concatenate([jnp.where(ch >= thr, 1.0, 0.0) for ch in _chunks(sc_ref[c])], axis=1)
        if hi <= past:
            selp_ref[:, lo:hi] = sel
        elif lo == past:
            seln_ref[...] = sel
        else:
            selp_ref[:, lo:past] = sel[:, 0:past - lo]
            seln_ref[...] = sel[:, past - lo:]


def _sample_select(sp, sn, rows, width, ktop):
    n, past = sp.shape
    spec = lambda w: pl.BlockSpec((rows, w), lambda i: (i, 0))
    return pl.pallas_call(
        functools.partial(_sample_select_kernel, width=width, ktop=ktop),
        grid=(n // rows,),
        in_specs=[spec(past), spec(LANES)],
        out_specs=[spec(past), spec(LANES)],
        out_shape=[jax.ShapeDtypeStruct((n, past), F32), jax.ShapeDtypeStruct((n, LANES), F32)],
        scratch_shapes=[pltpu.VMEM(((past + LANES) // width, rows, width), F32), pltpu.VMEM((rows, LANES), F32),
                        pltpu.VMEM((rows, LANES), F32), pltpu.VMEM((rows, LANES), F32)],
        compiler_params=_params("parallel"),
        name="sample_select",
    )(sp, sn)


def _sample_attn_kernel(pt_ref, q_ref, selp_ref, seln_ref, kn_ref, vn_ref, *rest, pages):
    kt_refs = rest[:pages]
    vt_refs = rest[pages:2 * pages]
    o_ref, m_ref, l_ref, acc_ref = rest[2 * pages:]
    step = pl.program_id(1)
    t_new = o_ref.shape[0]
    q = q_ref[...]

    def tile_rows(sel):
        return jnp.concatenate([sel] * N_HEADS_A, axis=0)

    def update(s, pv):
        m_old = m_ref[...]
        m_new = jnp.maximum(m_old, jnp.max(s, axis=1, keepdims=True))
        alpha = jnp.exp2(m_old - m_new)
        p = jnp.exp2(s - m_new)
        l_ref[...] = alpha * l_ref[...] + jnp.sum(p, axis=1, keepdims=True)
        acc_ref[...] = alpha * acc_ref[...] + pv(p.astype(BF16))
        m_ref[...] = m_new

    @pl.when(step == 0)
    def _():
        m_ref[...] = jnp.full(m_ref.shape, NEG, F32)
        l_ref[...] = jnp.zeros(l_ref.shape, F32)
        acc_ref[...] = jnp.zeros(acc_ref.shape, F32)
        s = jnp.where(tile_rows(seln_ref[...]) > 0.5, _dot_nt(q, kn_ref[...]), NEG)
        update(s, lambda p: _dot(p, vn_ref[...]))

    s = jnp.concatenate([_dot(q, kt_refs[i][...].astype(BF16)) for i in range(pages)], axis=1)
    s = jnp.where(tile_rows(selp_ref[...]) > 0.5, s, NEG)

    def pv(p):
        out = _dot_nt(p[:, 0:PAGE_SIZE], vt_refs[0][...].astype(BF16))
        for i in range(1, pages):
            out = out + _dot_nt(p[:, i * PAGE_SIZE:(i + 1) * PAGE_SIZE], vt_refs[i][...].astype(BF16))
        return out

    update(s, pv)

    @pl.when(step == pl.num_programs(1) - 1)
    def _():
        res = acc_ref[...] / l_ref[...]
        lane_head = lax.broadcasted_iota(jnp.int32, (1, D_ATTN), 1) // HEAD_DIM_A
        out = jnp.zeros((t_new, D_ATTN), F32)
        for h in range(N_HEADS_A):
            out = out + jnp.where(lane_head == h, res[h * t_new:(h + 1) * t_new], 0.0)
        o_ref[...] = out.astype(BF16)


def _sample_attn(layer, page_table, q_rows, selp, seln, kn_pad, vn_pad, cache_kt, cache_vt, n_pool, pages):
    batch, n_pages = page_table.shape
    ht = q_rows.shape[1]
    t_new = ht // N_HEADS_A
    page_spec = lambda i: pl.BlockSpec(
        (None, D_ATTN, PAGE_SIZE), lambda b, s, pt: (layer * n_pool + pt[b, s * pages + i], 0, 0))
    per_b = lambda r, w: pl.BlockSpec((None, r, w), lambda b, s, pt: (b, 0, 0))
    grid_spec = pltpu.PrefetchScalarGridSpec(
        num_scalar_prefetch=1,
        grid=(batch, n_pages // pages),
        in_specs=[per_b(ht, D_ATTN),
                  pl.BlockSpec((None, t_new, pages * PAGE_SIZE), lambda b, s, pt: (b, 0, s)),
                  per_b(t_new, LANES), per_b(LANES, D_ATTN), per_b(LANES, D_ATTN)]
        + [page_spec(i) for i in range(pages)] * 2,
        out_specs=per_b(t_new, D_ATTN),
        scratch_shapes=[pltpu.VMEM((ht, 1), F32), pltpu.VMEM((ht, 1), F32), pltpu.VMEM((ht, D_ATTN), F32)],
    )
    return pl.pallas_call(
        functools.partial(_sample_attn_kernel, pages=pages),
        grid_spec=grid_spec,
        out_shape=jax.ShapeDtypeStruct((batch, t_new, D_ATTN), BF16),
        compiler_params=_params("parallel", "arbitrary"),
        name="sample_attn",
    )(page_table, q_rows, selp, seln, kn_pad, vn_pad, *([cache_kt] * pages), *([cache_vt] * pages))


def _merge_kernel(x_ref, a_ref, g_ref, ga_ref, gb_ref, wa_ref, wb_ref, wo_ref, n2_ref, x1_ref, h2_ref, *, a_key_major):
    a_proj = _dot_tn(a_ref[...], wa_ref[...]) if a_key_major else _dot(a_ref[...], wa_ref[...])
    mixed = (jax.nn.sigmoid(ga_ref[...]) * a_proj
             + jax.nn.sigmoid(gb_ref[...]) * _dot(g_ref[...], wb_ref[...]))
    x1 = x_ref[...] + _dot(mixed.astype(BF16), wo_ref[...])
    x1_ref[...] = x1
    h2_ref[...] = (x1 * lax.rsqrt(jnp.mean(x1 * x1, axis=-1, keepdims=True) + EPS) * n2_ref[...]).astype(BF16)


def _merge(x2d, a_o, g_o, ga, gb, w_a, w_b, w_o, norm2, batch, tm, a_key_major):
    n, d = x2d.shape
    nt = n // batch // tm
    row = lambda w: pl.BlockSpec((tm, w), lambda b, i: (b * nt + i, 0))
    a_spec = pl.BlockSpec((None, D_ATTN, tm), lambda b, i: (b, 0, i)) if a_key_major else row(D_ATTN)
    weights = (w_a.astype(BF16), w_b.astype(BF16), w_o.astype(BF16))
    return pl.pallas_call(
        functools.partial(_merge_kernel, a_key_major=a_key_major),
        grid=(batch, nt),
        in_specs=[row(d), a_spec, row(D_GLA_V), row(d), row(d)]
        + [_const_spec(w.shape) for w in weights] + [_const_spec((1, d))],
        out_specs=[row(d), row(d)],
        out_shape=[jax.ShapeDtypeStruct((n, d), F32), jax.ShapeDtypeStruct((n, d), BF16)],
        compiler_params=_params("parallel", "parallel"),
        name="merge",
    )(x2d, a_o, g_o, ga, gb, *weights, norm2.reshape(1, d))


def _ffn_kernel(x1_ref, h2_ref, wg_ref, wu_ref, wd_ref, nf_ref, y_ref, *, n_split, final_norm):
    h2 = h2_ref[...]
    y = x1_ref[...]
    fc = wg_ref.shape[1] // n_split
    for i in range(n_split):
        g = _dot(h2, wg_ref[:, i * fc:(i + 1) * fc])
        u = _dot(h2, wu_ref[:, i * fc:(i + 1) * fc])
        act = (g * jax.nn.sigmoid(g) * u).astype(BF16)
        y = y + _dot(act, wd_ref[i * fc:(i + 1) * fc, :])
    if final_norm:
        y = y * lax.rsqrt(jnp.mean(y * y, axis=-1, keepdims=True) + EPS) * nf_ref[...]
    y_ref[...] = y


def _ffn(x1, h2, w_g, w_u, w_d, norm_final, tm, final_norm):
    n, d = x1.shape
    d_ff = w_g.shape[1]
    n_split = 2 if d_ff % (2 * LANES) == 0 else 1
    row = lambda: pl.BlockSpec((tm, d), lambda i: (i, 0))
    weights = (w_g.astype(BF16), w_u.astype(BF16), w_d.astype(BF16))
    return pl.pallas_call(
        functools.partial(_ffn_kernel, n_split=n_split, final_norm=final_norm),
        grid=(n // tm,),
        in_specs=[row(), row()] + [_const_spec(w.shape) for w in weights] + [_const_spec((1, d))],
        out_specs=row(),
        out_shape=jax.ShapeDtypeStruct((n, d), F32),
        compiler_params=_params("parallel"),
        name="ffn",
    )(x1, h2, *weights, norm_final.reshape(1, d))


def _pick(n, prefs):
    for p in prefs:
        if n % p == 0:
            return p
    return n


def kernel(x_prompt, x_sample, cache_k, cache_v, cache_idx_k, state_gla, page_table, norm1, w_in, w_gate_up,
           b_gate, g_gla_norm, w_branch_a, w_branch_b, w_out, norm2, w_ffn_gate, w_ffn_up, w_ffn_down, norm_final):
    depth = w_in.shape[0]
    bp, seq, d = x_prompt.shape
    bs, t_new, _ = x_sample.shape
    n_pool = cache_k.shape[1]
    n_pages = page_table.shape[1]
    past = n_pages * PAGE_SIZE
    xp = x_prompt.reshape(bp * seq, d)
    xs = x_sample.reshape(bs * t_new, d)
    cache_kt = cache_k.transpose(0, 1, 3, 4, 2).reshape(depth * n_pool, D_ATTN, PAGE_SIZE)
    cache_vt = cache_v.transpose(0, 1, 3, 4, 2).reshape(depth * n_pool, D_ATTN, PAGE_SIZE)
    cache_ikt = cache_idx_k.transpose(0, 1, 3, 2).reshape(depth * n_pool, D_IDX, PAGE_SIZE)
    outs = [[] for _ in range(8)]
    tq = _pick(seq, (256, 128))
    for l in range(depth):
        last = l == depth - 1
        (qkg, vg, la, r, ga, gb, qt, qit, wt, kb, kk, kt, vt, vtb, kit) = _inproj(
            xp, norm1[l], w_in[l], w_gate_up[l], b_gate[l], bp, _pick(seq, (256, 128)), True)
        a_o = _dsa_prompt(qt, qit, wt, kb, vtb, kk, bp, seq, tq)
        chunk = GLA_CHUNK if seq % GLA_CHUNK == 0 else seq
        g_o, st = _gla(qkg, vg, la, r, g_gla_norm[l], jnp.zeros((bp, DV_G, D_GLA_K), F32), bp, seq, chunk,
                       _pick(seq, (512, 256, 128, 64)), _pick(bp, (2, 1)))
        x1, h2 = _merge(xp, a_o, g_o, ga, gb, w_branch_a[l], w_branch_b[l], w_out[l], norm2[l], bp,
                        _pick(seq, (512, 256, 128)), True)
        xp = _ffn(x1, h2, w_ffn_gate[l], w_ffn_up[l], w_ffn_down[l], norm_final, _pick(bp * seq, (256, 128)), last)
        outs[0].append(kt.reshape(bp, N_HEADS_A, HEAD_DIM_A, seq).transpose(0, 3, 1, 2))
        outs[1].append(vt.reshape(bp, N_HEADS_A, HEAD_DIM_A, seq).transpose(0, 3, 1, 2))
        outs[2].append(kit.transpose(0, 2, 1))
        outs[3].append(_state_from_t(st))

        (qkg, vg, la, r, ga, gb, q, qi, kw, k, v, kb, vb, kk) = _inproj(
            xs, norm1[l], w_in[l], w_gate_up[l], b_gate[l], 1, _pick(bs * t_new, (256, 128)), False)
        a_rows = qi.reshape(bs, t_new, N_HEADS_IDX, D_IDX).transpose(0, 2, 1, 3).reshape(bs, N_HEADS_IDX * t_new, D_IDX)
        w_rows = (kw[:, D_IDX:D_IDX + N_HEADS_IDX] * (N_HEADS_IDX ** -0.5 * D_IDX ** -0.5)).reshape(
            bs, t_new, N_HEADS_IDX).transpose(0, 2, 1).reshape(bs, N_HEADS_IDX * t_new, 1)
        w_rows = jnp.broadcast_to(w_rows, (bs, N_HEADS_IDX * t_new, LANES))
        pad_rows = lambda a: jnp.pad(a.reshape(bs, t_new, a.shape[-1]), ((0, 0), (0, LANES - t_new), (0, 0)))
        sp, sn = _sample_scores(l, page_table, a_rows, w_rows, pad_rows(kk[:, 0:D_IDX]), cache_ikt, n_pool,
                                _pick(n_pages, (32, 16, 8, 4, 2, 1)))
        ktop = min(TOPK_MAX, (past + t_new) // 4)
        width = 5 * LANES if (past + LANES) % (5 * LANES) == 0 else LANES
        rows = _pick(bs * t_new, (128, 64, 32, 16, 8))
        selp, seln = _sample_select(sp.reshape(bs * t_new, past), sn.reshape(bs * t_new, LANES), rows, width, ktop)
        head_of_lane = jnp.arange(D_ATTN) // HEAD_DIM_A
        q_rows = jnp.where(head_of_lane[None, None, None, :] == jnp.arange(N_HEADS_A)[None, :, None, None],
                           q.reshape(bs, 1, t_new, D_ATTN), jnp.zeros((), BF16)).reshape(bs, N_HEADS_A * t_new, D_ATTN)
        a_o = _sample_attn(l, page_table, q_rows, selp.reshape(bs, t_new, past), seln.reshape(bs, t_new, LANES),
                           pad_rows(kb), pad_rows(vb), cache_kt, cache_vt, n_pool, _pick(n_pages, (16, 8, 4, 2, 1)))
        gchunk = 16
        pad_g = lambda a: jnp.pad(a.reshape(bs, t_new, a.shape[-1]), ((0, 0), (0, gchunk - t_new), (0, 0))).reshape(
            bs * gchunk, a.shape[-1])
        g_o, st = _gla(pad_g(qkg), pad_g(vg), pad_g(la), pad_g(r), g_gla_norm[l], _state_to_t(state_gla[l].astype(F32)),
                       bs, gchunk, gchunk, gchunk, _pick(bs, (8, 4, 2, 1)))
        g_o = g_o.reshape(bs, gchunk, D_GLA_V)[:, 0:t_new].reshape(bs * t_new, D_GLA_V)
        x1, h2 = _merge(xs, a_o.reshape(bs * t_new, D_ATTN), g_o, ga, gb, w_branch_a[l], w_branch_b[l], w_out[l],
                        norm2[l], 1, _pick(bs * t_new, (512, 256, 128)), False)
        xs = _ffn(x1, h2, w_ffn_gate[l], w_ffn_up[l], w_ffn_down[l], norm_final, _pick(bs * t_new, (256, 128)), last)
        outs[4].append(k.reshape(bs, t_new, N_HEADS_A, HEAD_DIM_A))
        outs[5].append(v.reshape(bs, t_new, N_HEADS_A, HEAD_DIM_A))
        outs[6].append(kw[:, 0:D_IDX].reshape(bs, t_new, D_IDX))
        outs[7].append(_state_from_t(st))
    return (xp.reshape(bp, seq, d), xs.reshape(bs, t_new, d)) + tuple(jnp.stack(o) for o in outs)
```

```python
import functools

import jax
import jax.numpy as jnp
from jax import lax
from jax.experimental import pallas as pl
from jax.experimental.pallas import tpu as pltpu

F32 = jnp.float32
BF16 = jnp.bfloat16

D_ATTN = 512
N_HEADS_A = 8
HEAD_DIM_A = 64
N_HEADS_IDX = 8
D_IDX = 64
TOPK_MAX = 256
N_HEADS_G = 4
DK_G = 64
DV_G = 128
D_GLA_K = N_HEADS_G * DK_G
D_GLA_V = N_HEADS_G * DV_G
GATE_RANK = 16
GATE_TAU = 16.0
GLA_CHUNK = 64
EPS = 1e-6
PAGE_SIZE = 128

LANES = 128
NEG = -1e30
LOWEST = -3.0e38
MAX_BISECT = 64
Q_SCALE = HEAD_DIM_A ** -0.5 * 1.4426950408889634
VMEM_LIMIT = 56 * 1024 * 1024


def _dot(a, b):
    return jnp.dot(a, b, preferred_element_type=F32)


def _dot_nt(a, b):
    return lax.dot_general(a, b, (((1,), (1,)), ((), ())), preferred_element_type=F32)


def _dot_tn(a, b):
    return lax.dot_general(a, b, (((0,), (0,)), ((), ())), preferred_element_type=F32)


def _params(*sem):
    return pltpu.CompilerParams(dimension_semantics=sem, vmem_limit_bytes=VMEM_LIMIT)


def _const_spec(shape):
    n = len(shape)
    return pl.BlockSpec(shape, lambda *_: (0,) * n)


def _chunks(x):
    return [x[:, i * LANES:(i + 1) * LANES] for i in range(x.shape[1] // LANES)]


def _rep(col, rows):
    return jnp.broadcast_to(col, (rows, LANES))


def _inproj_kernel(x_ref, g_ref, wc_ref, wd_ref, we_ref, wgu_ref, bg_ref, *refs, key_major):
    x = x_ref[...]
    h = (x * lax.rsqrt(jnp.mean(x * x, axis=-1, keepdims=True) + EPS) * g_ref[...]).astype(BF16)
    if key_major:
        (wqt_ref, wkvt_ref, wkk_ref, wkit_ref, wwit_ref,
         qkg_ref, vg_ref, la_ref, r_ref, ga_ref, gb_ref,
         qt_ref, qit_ref, wt_ref, kb_ref, kk_ref, kt_ref, vt_ref, vtb_ref, kit_ref) = refs
        qq = _dot_nt(wqt_ref[...], h)
        qt_ref[...] = (qq[0:512] * Q_SCALE).astype(BF16)
        qit_ref[...] = qq[512:1024].astype(BF16)
        kv = _dot_nt(wkvt_ref[...], h)
        kt_ref[...] = kv[0:512]
        vt_ref[...] = kv[512:1024]
        vtb_ref[...] = kv[512:1024].astype(BF16)
        kb_ref[...] = kv[0:512].T.astype(BF16)
        kk_ref[...] = _dot(h, wkk_ref[...]).astype(BF16)
        kit_ref[...] = _dot_nt(wkit_ref[...], h)
        wt_ref[...] = _dot_nt(wwit_ref[...], h)[0:N_HEADS_IDX] * (N_HEADS_IDX ** -0.5 * D_IDX ** -0.5)
    else:
        (wq_ref, wkv_ref, wkk_ref, wkw_ref,
         qkg_ref, vg_ref, la_ref, r_ref, ga_ref, gb_ref,
         q_ref, qi_ref, kw_ref, k_ref, v_ref, kb_ref, vb_ref, kk_ref) = refs
        a = _dot(h, wq_ref[...])
        q_ref[...] = (a[:, 0:512] * Q_SCALE).astype(BF16)
        qi_ref[...] = a[:, 512:1024].astype(BF16)
        kw_ref[...] = _dot(h, wkw_ref[...])
        kv = _dot(h, wkv_ref[...])
        k_ref[...] = kv[:, 0:512]
        v_ref[...] = kv[:, 512:1024]
        kb_ref[...] = kv[:, 0:512].astype(BF16)
        vb_ref[...] = kv[:, 512:1024].astype(BF16)
        kk_ref[...] = _dot(h, wkk_ref[...]).astype(BF16)
    c = _dot(h, wc_ref[...])
    qkg_ref[...] = c[:, 0:512]
    vg_ref[...] = c[:, 512:1024]
    glr = _dot(h, wd_ref[...]).astype(BF16)
    z = _dot(glr, wgu_ref[...]) + bg_ref[...]
    la_ref[...] = (jnp.minimum(z, 0.0) - jnp.log1p(jnp.exp(-jnp.abs(z)))) * (1.0 / GATE_TAU)
    e = _dot(h, we_ref[...])
    r_ref[...] = e[:, 0:512]
    ga_ref[...] = jax.nn.sigmoid(e[:, 512:1536]).astype(BF16)
    gb_ref[...] = jax.nn.sigmoid(e[:, 1536:2560]).astype(BF16)


def _inproj(x2d, norm_g, w_in, w_gate_up, b_gate, batch, tm, key_major):
    n, d = x2d.shape
    t_len = n // batch
    nt = t_len // tm
    wb = w_in.astype(BF16)
    wq = jnp.concatenate([wb[:, 0:512], wb[:, 1536:2048]], axis=1)
    wkv = wb[:, 512:1536]
    wki = wb[:, 2048:2112]
    wwi = wb[:, 2112:2120]
    wkk = jnp.concatenate([wki, wki], axis=1)
    wc = wb[:, 2120:3144]
    wd = jnp.concatenate([wb[:, 3144:3160], jnp.zeros((d, LANES - GATE_RANK), BF16)], axis=1)
    we = wb[:, 3160:5720]
    wgu = jnp.concatenate([w_gate_up.astype(BF16), jnp.zeros((LANES - GATE_RANK, D_GLA_K), BF16)], axis=0)
    row = lambda w: pl.BlockSpec((tm, w), lambda b, i: (b * nt + i, 0))
    col = lambda r: pl.BlockSpec((None, r, tm), lambda b, i: (b, 0, i))
    rows_out = lambda outs: ([row(w) for w, _ in outs], [jax.ShapeDtypeStruct((n, w), dt) for w, dt in outs])
    cols_out = lambda outs: ([col(r) for r, _ in outs],
                             [jax.ShapeDtypeStruct((batch, r, t_len), dt) for r, dt in outs])
    out_specs, out_shape = rows_out([(512, F32), (512, F32), (256, F32), (512, F32), (1024, BF16), (1024, BF16)])
    if key_major:
        wwit = jnp.concatenate([wwi.T, jnp.zeros((16 - N_HEADS_IDX, d), BF16)], axis=0)
        flavor_w = (wq.T, wkv.T, wkk, wki.T, wwit)
        s1, h1 = cols_out([(512, BF16), (512, BF16), (N_HEADS_IDX, F32)])
        s2, h2 = rows_out([(512, BF16), (LANES, BF16)])
        s3, h3 = cols_out([(512, F32), (512, F32), (512, BF16), (D_IDX, F32)])
        out_specs, out_shape = out_specs + s1 + s2 + s3, out_shape + h1 + h2 + h3
    else:
        wkw = jnp.concatenate([wki, wwi, jnp.zeros((d, LANES - 72), BF16)], axis=1)
        flavor_w = (wq, wkv, wkk, wkw)
        s1, h1 = rows_out([(512, BF16), (512, BF16), (LANES, F32), (512, F32), (512, F32), (512, BF16),
                           (512, BF16), (LANES, BF16)])
        out_specs, out_shape = out_specs + s1, out_shape + h1
    weights = (wc, wd, we, wgu)
    return pl.pallas_call(
        functools.partial(_inproj_kernel, key_major=key_major),
        grid=(batch, nt),
        in_specs=[row(d), _const_spec((1, d))] + [_const_spec(w.shape) for w in weights]
        + [_const_spec((1, D_GLA_K))] + [_const_spec(w.shape) for w in flavor_w],
        out_specs=out_specs,
        out_shape=out_shape,
        compiler_params=_params("parallel", "parallel"),
        name="inproj",
    )(x2d, norm_g.reshape(1, d), *weights, b_gate.reshape(1, D_GLA_K), *flavor_w)


def _gla_kernel(qk_ref, v_ref, la_ref, r_ref, gn_ref, s0_ref, go_ref, sfin_ref, st_ref, *, chunk, n_chunks):
    step = pl.program_id(1)

    @pl.when(step == 0)
    def _():
        st_ref[...] = s0_ref[...]

    ri = lax.broadcasted_iota(jnp.int32, (chunk, chunk), 0)
    ci = lax.broadcasted_iota(jnp.int32, (chunk, chunk), 1)
    tril = ci <= ri
    tri = jnp.where(tril, 1.0, 0.0).astype(F32)
    lane_head = lax.broadcasted_iota(jnp.int32, (1, D_GLA_K), 1) // DK_G
    mid = chunk // 2 - 1
    gn = gn_ref[...]

    def body(c, carry):
        for g in range(st_ref.shape[0]):
            one_chunk(c, g)
        return carry

    def one_chunk(c, g):
        r0 = pl.multiple_of(c * chunk, chunk)
        qk = qk_ref[g, pl.ds(r0, chunk), :]
        q = qk[:, 0:D_GLA_K] * DK_G ** -0.5
        k = qk[:, D_GLA_K:2 * D_GLA_K]
        la = la_ref[g, pl.ds(r0, chunk), :]
        b = jnp.dot(tri, la, precision=lax.Precision.HIGHEST, preferred_element_type=F32)
        b_mid = b[mid:mid + 1, :]
        b_end = b[chunk - 1:chunk, :]
        qe = q * jnp.exp(b)
        qt = q * jnp.exp(b - b_mid)
        ktb = (k * jnp.exp(b_mid - b)).astype(BF16)
        kdb = (k * jnp.exp(b_end - b)).astype(BF16)
        st = st_ref[g]
        stb = st.astype(BF16)
        new_st = st * jnp.exp(b_end)
        for h in range(N_HEADS_G):
            hm = lane_head == h
            qt_h = jnp.where(hm, qt, 0.0).astype(BF16)
            qe_h = jnp.where(hm, qe, 0.0).astype(BF16)
            att = jnp.where(tril, _dot_nt(qt_h, ktb), 0.0)
            vb = v_ref[g, pl.ds(r0, chunk), h * DV_G:(h + 1) * DV_G].astype(BF16)
            o = _dot_nt(qe_h, stb) + _dot(att.astype(BF16), vb)
            new_st = new_st + jnp.where(hm, _dot_tn(vb, kdb), 0.0)
            on = o * lax.rsqrt(jnp.mean(o * o, axis=-1, keepdims=True) + EPS) * gn
            rr = r_ref[g, pl.ds(r0, chunk), h * DV_G:(h + 1) * DV_G]
            go_ref[g, pl.ds(r0, chunk), h * DV_G:(h + 1) * DV_G] = (on * (rr * jax.nn.sigmoid(rr))).astype(BF16)
        st_ref[g] = new_st

    lax.fori_loop(0, n_chunks, body, 0)

    @pl.when(step == pl.num_programs(1) - 1)
    def _():
        sfin_ref[...] = st_ref[...]


def _gla(qkg, vg, la, r, g_norm, s0_t, batch, t_len, chunk, rows_per_step, group):
    n_steps = t_len // rows_per_step
    row = lambda w: pl.BlockSpec((group, rows_per_step, w), lambda b, s: (b, s, 0))
    st_spec = pl.BlockSpec((group, DV_G, D_GLA_K), lambda b, s: (b, 0, 0))
    seqs = lambda a: a.reshape(batch, t_len, a.shape[-1])
    g_o, st = pl.pallas_call(
        functools.partial(_gla_kernel, chunk=chunk, n_chunks=rows_per_step // chunk),
        grid=(batch // group, n_steps),
        in_specs=[row(512), row(512), row(256), row(512), _const_spec((1, DV_G)), st_spec],
        out_specs=[row(512), st_spec],
        out_shape=[jax.ShapeDtypeStruct((batch, t_len, D_GLA_V), BF16),
                   jax.ShapeDtypeStruct((batch, DV_G, D_GLA_K), F32)],
        scratch_shapes=[pltpu.VMEM((group, DV_G, D_GLA_K), F32)],
        compiler_params=_params("parallel", "arbitrary"),
        name="gla",
    )(seqs(qkg), seqs(vg), seqs(la), seqs(r), g_norm.reshape(1, DV_G), s0_t)
    return g_o.reshape(batch * t_len, D_GLA_V), st


def _state_to_t(s):
    b = s.shape[0]
    return s.transpose(0, 3, 1, 2).reshape(b, DV_G, D_GLA_K)


def _state_from_t(st):
    b = st.shape[0]
    return st.reshape(b, DV_G, N_HEADS_G, DK_G).transpose(0, 2, 3, 1)


def _topk_threshold(get_tile, set_tile, n_tiles, rows, width, ktop, lo_ref, hi_ref, flo_ref):
    kf = float(ktop)
    one = lambda m: jnp.where(m, 1.0, 0.0)
    zeros = jnp.zeros((rows, LANES), F32)
    rsum = lambda a: _rep(jnp.sum(a, axis=1, keepdims=True), rows)

    def acc_chunks(blk, acc, fn, op):
        for ch in _chunks(blk):
            acc = op(acc, fn(ch))
        return acc

    def stats(c, st):
        mx, mn, nv, c0, c0e = st
        blk = get_tile(c)
        mx = acc_chunks(blk, mx, lambda ch: ch, jnp.maximum)
        mn = acc_chunks(blk, mn, lambda ch: jnp.where(ch > -jnp.inf, ch, jnp.inf), jnp.minimum)
        nv = acc_chunks(blk, nv, lambda ch: one(ch > -jnp.inf), jnp.add)
        c0 = acc_chunks(blk, c0, lambda ch: one(ch > 0.0), jnp.add)
        c0e = acc_chunks(blk, c0e, lambda ch: one(ch >= 0.0), jnp.add)
        return mx, mn, nv, c0, c0e

    mx, mn, nv, c0, c0e = lax.fori_loop(
        0, n_tiles, stats,
        (jnp.full((rows, LANES), -jnp.inf, F32), jnp.full((rows, LANES), jnp.inf, F32), zeros, zeros, zeros))
    rowmax = _rep(jnp.max(mx, axis=1, keepdims=True), rows)
    rowmin = _rep(jnp.min(mn, axis=1, keepdims=True), rows)
    nvalid, cnt0, cnt0e = rsum(nv), rsum(c0), rsum(c0e)

    def count_gt(thr):
        def body(c, acc):
            return acc_chunks(get_tile(c), acc, lambda ch: one(ch > thr), jnp.add)
        return rsum(lax.fori_loop(0, n_tiles, body, zeros))

    small = nvalid <= kf
    ztie = jnp.logical_and(jnp.logical_and(cnt0 < kf, cnt0e >= kf), jnp.logical_not(small))
    frozen = jnp.logical_or(small, ztie)
    pos = cnt0 >= kf
    lo_ref[...] = jnp.where(frozen, 0.0, jnp.where(pos, 0.0, jnp.minimum(rowmin - 1.0, rowmin * 2.0)))
    hi_ref[...] = jnp.where(frozen, 0.0, jnp.where(pos, rowmax, 0.0))
    flo_ref[...] = jnp.where(frozen, kf, jnp.where(pos, cnt0, nvalid))

    def cond(st):
        it, pending = st
        return jnp.logical_and(it < MAX_BISECT, pending > 0)

    def bisect(st):
        it, _ = st
        lo, hi = lo_ref[...], hi_ref[...]
        mid = 0.5 * (lo + hi)
        c = count_gt(mid)
        ge = c >= kf
        flo = jnp.where(ge, c, flo_ref[...])
        lo_ref[...] = jnp.where(ge, mid, lo)
        hi_ref[...] = jnp.where(ge, hi, mid)
        flo_ref[...] = flo
        return it + 1, jnp.max(one(flo != kf)).astype(jnp.int32)

    pending0 = jnp.max(one(flo_ref[...] != kf)).astype(jnp.int32)
    lax.while_loop(cond, bisect, (jnp.int32(0), pending0))

    lo = lo_ref[...]

    def tmin(c, acc):
        return acc_chunks(get_tile(c), acc, lambda ch: jnp.where(ch > lo, ch, jnp.inf), jnp.minimum)

    thr = _rep(jnp.min(lax.fori_loop(0, n_tiles, tmin, jnp.full((rows, LANES), jnp.inf, F32)),
                       axis=1, keepdims=True), rows)
    thr = jnp.where(ztie, 0.0, thr)
    thr = jnp.where(small, LOWEST, thr)

    def gteq(c, st):
        g, e = st
        blk = get_tile(c)
        return (acc_chunks(blk, g, lambda ch: one(ch > thr), jnp.add),
                acc_chunks(blk, e, lambda ch: one(ch == thr), jnp.add))

    g, e = lax.fori_loop(0, n_tiles, gteq, (zeros, zeros))
    need = kf - rsum(g)
    any_tie = jnp.max(one(rsum(e) > need)) > 0.0

    @pl.when(any_tie)
    def _():
        ri = lax.broadcasted_iota(jnp.int32, (width, width), 0)
        ci = lax.broadcasted_iota(jnp.int32, (width, width), 1)
        upper = jnp.where(ri <= ci, 1.0, 0.0).astype(BF16)
        thr_c, need_c = thr[:, 0:1], need[:, 0:1]

        def drop(c, seen):
            blk = get_tile(c)
            eq = blk == thr_c
            eqf = one(eq)
            rank = _dot(eqf.astype(BF16), upper) + seen - 1.0
            set_tile(c, jnp.where(jnp.logical_and(eq, rank >= need_c), -jnp.inf, blk))
            return seen + jnp.sum(eqf, axis=1, keepdims=True)

        lax.fori_loop(0, n_tiles, drop, jnp.zeros((rows, 1), F32))

    return thr


SUB = 8


def _groups(x):
    return [x[i * SUB:(i + 1) * SUB] for i in range(x.shape[0] // SUB)]


def _tree(op, xs):
    xs = list(xs)
    while len(xs) > 1:
        xs = [op(xs[i], xs[i + 1]) if i + 1 < len(xs) else xs[i] for i in range(0, len(xs), 2)]
    return xs[0]


def _col(op, x8):
    return jnp.broadcast_to(op(x8, axis=0, keepdims=True), x8.shape)


def _col_stats_init(n):
    zeros = jnp.zeros((SUB, n), F32)
    return jnp.full((SUB, n), -jnp.inf, F32), jnp.full((SUB, n), jnp.inf, F32), zeros, zeros, zeros


def _col_stats_update(st, tile):
    one = lambda m: jnp.where(m, 1.0, 0.0)
    mx, mn, nv, c0, c0e = st
    g = _groups(tile)
    mx = jnp.maximum(mx, _tree(jnp.maximum, g))
    mn = jnp.minimum(mn, _tree(jnp.minimum, [jnp.where(x > -jnp.inf, x, jnp.inf) for x in g]))
    nv = nv + _tree(jnp.add, [one(x > -jnp.inf) for x in g])
    c0 = c0 + _tree(jnp.add, [one(x > 0.0) for x in g])
    c0e = c0e + _tree(jnp.add, [one(x >= 0.0) for x in g])
    return mx, mn, nv, c0, c0e


def _topk_cut_cols(get_tile, set_tile, n_tiles, tk, n, ktop, stats, lo_ref, hi_ref, flo_ref):
    kf = float(ktop)
    one = lambda m: jnp.where(m, 1.0, 0.0)
    zeros = jnp.zeros((SUB, n), F32)
    total = lambda a: _col(jnp.sum, a)
    mx, mn, nv, c0, c0e = stats
    colmax, colmin = _col(jnp.max, mx), _col(jnp.min, mn)
    nvalid, cnt0, cnt0e = total(nv), total(c0), total(c0e)

    def count_gt(thr):
        def body(c, acc):
            return acc + _tree(jnp.add, [one(x > thr) for x in _groups(get_tile(c))])
        return total(lax.fori_loop(0, n_tiles, body, zeros))

    small = nvalid <= kf
    ztie = jnp.logical_and(jnp.logical_and(cnt0 < kf, cnt0e >= kf), jnp.logical_not(small))
    frozen = jnp.logical_or(small, ztie)
    pos = cnt0 >= kf
    lo_ref[...] = jnp.where(frozen, 0.0, jnp.where(pos, 0.0, jnp.minimum(colmin - 1.0, colmin * 2.0)))
    hi_ref[...] = jnp.where(frozen, 0.0, jnp.where(pos, colmax, 0.0))
    flo_ref[...] = jnp.where(frozen, kf, jnp.where(pos, cnt0, nvalid))

    def cond(st):
        it, pending = st
        return jnp.logical_and(it < MAX_BISECT, pending > 0)

    def bisect_until(slack):
        def bisect(st):
            it, _ = st
            lo, hi = lo_ref[...], hi_ref[...]
            mid = 0.5 * (lo + hi)
            c = count_gt(mid)
            ge = c >= kf
            flo = jnp.where(ge, c, flo_ref[...])
            lo_ref[...] = jnp.where(ge, mid, lo)
            hi_ref[...] = jnp.where(ge, hi, mid)
            flo_ref[...] = flo
            return it + 1, jnp.max(one(flo > kf + slack)).astype(jnp.int32)
        return bisect

    def next_above(lo):
        def body(c, acc):
            return jnp.minimum(
                acc, _tree(jnp.minimum, [jnp.where(x > lo, x, jnp.inf) for x in _groups(get_tile(c))]))
        return _col(jnp.min, lax.fori_loop(0, n_tiles, body, jnp.full((SUB, n), jnp.inf, F32)))

    pending0 = jnp.max(one(flo_ref[...] > kf + 1.0)).astype(jnp.int32)
    it1, _ = lax.while_loop(cond, bisect_until(1.0), (jnp.int32(0), pending0))
    lo1, flo1 = lo_ref[...], flo_ref[...]
    lo_adv = jnp.where(flo1 == kf + 1.0, next_above(lo1), lo1)
    stepped = count_gt(lo_adv) == kf
    lo_ref[...] = jnp.where(stepped, lo_adv, lo1)
    flo_ref[...] = jnp.where(stepped, kf, flo1)
    pending1 = jnp.max(one(flo_ref[...] != kf)).astype(jnp.int32)
    lax.while_loop(cond, bisect_until(0.0), (it1, pending1))

    lo = lo_ref[...]
    unresolved = jnp.logical_or(ztie, flo_ref[...] != kf)
    lo_ref[...] = jnp.where(small, -jnp.inf, lo)

    @pl.when(jnp.max(one(unresolved)) > 0.0)
    def _():
        thr = jnp.where(ztie, 0.0, next_above(lo))

        def gt(c, g):
            return g + _tree(jnp.add, [one(x > thr) for x in _groups(get_tile(c))])

        need = kf - total(lax.fori_loop(0, n_tiles, gt, zeros))
        ri = lax.broadcasted_iota(jnp.int32, (tk, tk), 0)
        ci = lax.broadcasted_iota(jnp.int32, (tk, tk), 1)
        lower = jnp.where(ci <= ri, 1.0, 0.0).astype(BF16)
        thr_r, need_r, unres_r = thr[0:1], need[0:1], unresolved[0:1]

        def drop(c, st):
            seen, below = st
            blk = get_tile(c)
            eq = jnp.logical_and(blk == thr_r, unres_r)
            eqf = one(eq)
            rank = _dot(lower, eqf.astype(BF16)) + seen - 1.0
            set_tile(c, jnp.where(jnp.logical_and(eq, rank >= need_r), -jnp.inf, blk))
            below = jnp.maximum(below, jnp.max(jnp.where(blk < thr_r, blk, -jnp.inf), axis=0, keepdims=True))
            return seen + jnp.sum(eqf, axis=0, keepdims=True), below

        _, below = lax.fori_loop(0, n_tiles, drop, (jnp.zeros((1, n), F32), jnp.full((1, n), -jnp.inf, F32)))
        lo_ref[...] = jnp.where(unresolved, jnp.broadcast_to(below, (SUB, n)), lo_ref[...])

    return lo_ref[...]


def _dsa_prompt_kernel(qt_ref, qit_ref, wt_ref, k_ref, vt_ref, kk_ref, o_ref,
                       sc_ref, lo_ref, hi_ref, flo_ref, m_ref, l_ref, acc_ref, sa_ref, sb_ref, *, tq, ktop):
    j = pl.program_id(1)
    n_tiles = j + 1
    t0 = j * tq
    qpos = t0 + lax.broadcasted_iota(jnp.int32, (1, tq), 1)
    kiota = lax.broadcasted_iota(jnp.int32, (tq, 1), 0)
    per = LANES // HEAD_DIM_A
    reps = tq // SUB
    zero_rows = jnp.zeros((LANES - D_IDX, tq), BF16)

    def tall(x8):
        return pltpu.repeat(x8, reps, axis=0)

    qit = qit_ref[...]
    wt = wt_ref[...]
    qih = [jnp.concatenate([qit[h * D_IDX:(h + 1) * D_IDX], zero_rows], axis=0) for h in range(N_HEADS_IDX)]
    wrow = [tall(jnp.broadcast_to(wt[h:h + 1], (SUB, tq))) for h in range(N_HEADS_IDX)]

    def score_tile(c, stats):
        k0 = pl.multiple_of(c * tq, tq)
        kk = kk_ref[pl.ds(k0, tq), :]
        acc = jnp.zeros((tq, tq), F32)
        for h in range(N_HEADS_IDX):
            acc = acc + jnp.maximum(_dot(kk, qih[h]), 0.0) * wrow[h]
        tile = jnp.where(k0 + kiota <= qpos, acc, -jnp.inf)
        sc_ref[c] = tile
        return _col_stats_update(stats, tile)

    stats = lax.fori_loop(0, n_tiles, score_tile, _col_stats_init(tq))

    def get_tile(c):
        return sc_ref[c]

    def set_tile(c, val):
        sc_ref[c] = val

    cut = tall(_topk_cut_cols(get_tile, set_tile, n_tiles, tq, tq, ktop, stats, lo_ref, hi_ref, flo_ref))

    qt = qt_ref[...]
    zq = jnp.zeros((HEAD_DIM_A, tq), BF16)
    qh = []
    for h in range(N_HEADS_A):
        rows = qt[h * HEAD_DIM_A:(h + 1) * HEAD_DIM_A]
        qh.append(jnp.concatenate([zq] * (h % per) + [rows] + [zq] * (per - 1 - h % per), axis=0))
    m_ref[...] = jnp.full(m_ref.shape, NEG, F32)
    l_ref[...] = jnp.zeros(l_ref.shape, F32)
    acc_ref[...] = jnp.zeros(acc_ref.shape, F32)

    ta = tq // 2
    cut_a = cut[0:ta]
    ones_rows = jnp.ones((2 * SUB, ta), BF16)
    last = 2 * n_tiles - 1

    def logits(c, buf):
        k0 = pl.multiple_of(c * ta, ta)
        for h in range(N_HEADS_A):
            buf[h] = _dot(k_ref[pl.ds(k0, ta), (h // per) * LANES:(h // per + 1) * LANES], qh[h])

    def softmax_pv(c, half, buf):
        k0 = pl.multiple_of(c * ta, ta)
        bias = jnp.where(sc_ref[c // 2, half * ta:(half + 1) * ta, :] > cut_a, 0.0, NEG)
        for h in range(N_HEADS_A):
            s = buf[h] + bias
            vh = jnp.concatenate([vt_ref[h * HEAD_DIM_A:(h + 1) * HEAD_DIM_A, pl.ds(k0, ta)], ones_rows], axis=0)
            m_old = m_ref[h]
            m_new = jnp.maximum(m_old, _col(jnp.max, _tree(jnp.maximum, _groups(s))))
            alpha = jnp.exp2(m_old - m_new)
            p = jnp.exp2(s - pltpu.repeat(m_new, ta // SUB, axis=0))
            pv = _dot(vh, p.astype(BF16))
            l_ref[h] = alpha * l_ref[h] + pv[HEAD_DIM_A:HEAD_DIM_A + SUB]
            acc_ref[h] = pltpu.repeat(alpha, HEAD_DIM_A // SUB, axis=0) * acc_ref[h] + pv[0:HEAD_DIM_A]
            m_ref[h] = m_new

    logits(0, sa_ref)

    def att_tile(i, carry):
        logits(2 * i + 1, sb_ref)
        softmax_pv(2 * i, 0, sa_ref)
        logits(jnp.minimum(2 * i + 2, last), sa_ref)
        softmax_pv(2 * i + 1, 1, sb_ref)
        return carry

    lax.fori_loop(0, n_tiles, att_tile, 0)

    for h in range(N_HEADS_A):
        o_ref[h * HEAD_DIM_A:(h + 1) * HEAD_DIM_A, :] = (
            acc_ref[h] / pltpu.repeat(l_ref[h], HEAD_DIM_A // SUB, axis=0)).astype(BF16)


def _dsa_prompt(qt, qit, wt, kb, vtb, kk, batch, seq, tq):
    nq = seq // tq
    ktop = min(TOPK_MAX, seq // 4)
    qcol = lambda r: pl.BlockSpec((None, r, tq), lambda b, j: (b, 0, j))
    rows = lambda w: pl.BlockSpec((seq, w), lambda b, j: (b, 0))
    return pl.pallas_call(
        functools.partial(_dsa_prompt_kernel, tq=tq, ktop=ktop),
        grid=(batch, nq),
        in_specs=[qcol(512), qcol(512), qcol(N_HEADS_IDX), rows(512),
                  pl.BlockSpec((None, 512, seq), lambda b, j: (b, 0, 0)), rows(LANES)],
        out_specs=qcol(512),
        out_shape=jax.ShapeDtypeStruct((batch, D_ATTN, seq), BF16),
        scratch_shapes=[pltpu.VMEM((nq, tq, tq), F32), pltpu.VMEM((SUB, tq), F32), pltpu.VMEM((SUB, tq), F32),
                        pltpu.VMEM((SUB, tq), F32), pltpu.VMEM((N_HEADS_A, SUB, tq), F32),
                        pltpu.VMEM((N_HEADS_A, SUB, tq), F32), pltpu.VMEM((N_HEADS_A, HEAD_DIM_A, tq), F32),
                        pltpu.VMEM((N_HEADS_A, tq // 2, tq), F32), pltpu.VMEM((N_HEADS_A, tq // 2, tq), F32)],
        compiler_params=_params("parallel", "arbitrary"),
        name="dsa_prompt",
    )(qt, qit, wt, kb, vtb, kk)


def _sample_scores_kernel(pt_ref, a_ref, w_ref, kin_ref, *rest, pages):
    page_refs = rest[:pages]
    sp_ref, sn_ref = rest[pages], rest[pages + 1]
    t_new = sn_ref.shape[0]
    a = a_ref[...]
    w = w_ref[...]

    def head_sum(s):
        r = jnp.maximum(s, 0.0) * w
        out = r[0:t_new]
        for h in range(1, N_HEADS_IDX):
            out = out + r[h * t_new:(h + 1) * t_new]
        return out

    for i in range(pages):
        ikt = page_refs[i][...].astype(BF16)
        sp_ref[:, i * PAGE_SIZE:(i + 1) * PAGE_SIZE] = head_sum(_dot(a, ikt))

    @pl.when(pl.program_id(1) == 0)
    def _():
        sn = head_sum(_dot_nt(a, kin_ref[...]))
        ti = lax.broadcasted_iota(jnp.int32, sn.shape, 0)
        si = lax.broadcasted_iota(jnp.int32, sn.shape, 1)
        sn_ref[...] = jnp.where(si <= ti, sn, -jnp.inf)


def _sample_scores(layer, page_table, a_rows, w_rows, ki_new_pad, cache_idx_kt, n_pool, pages):
    batch, n_pages = page_table.shape
    ht = a_rows.shape[1]
    t_new = ht // N_HEADS_IDX
    page_spec = lambda i: pl.BlockSpec(
        (None, D_IDX, PAGE_SIZE), lambda b, s, pt: (layer * n_pool + pt[b, s * pages + i], 0, 0))
    grid_spec = pltpu.PrefetchScalarGridSpec(
        num_scalar_prefetch=1,
        grid=(batch, n_pages // pages),
        in_specs=[pl.BlockSpec((None, ht, D_IDX), lambda b, s, pt: (b, 0, 0)),
                  pl.BlockSpec((None, ht, LANES), lambda b, s, pt: (b, 0, 0)),
                  pl.BlockSpec((None, LANES, D_IDX), lambda b, s, pt: (b, 0, 0))]
        + [page_spec(i) for i in range(pages)],
        out_specs=[pl.BlockSpec((None, t_new, pages * PAGE_SIZE), lambda b, s, pt: (b, 0, s)),
                   pl.BlockSpec((None, t_new, LANES), lambda b, s, pt: (b, 0, 0))],
    )
    return pl.pallas_call(
        functools.partial(_sample_scores_kernel, pages=pages),
        grid_spec=grid_spec,
        out_shape=[jax.ShapeDtypeStruct((batch, t_new, n_pages * PAGE_SIZE), F32),
                   jax.ShapeDtypeStruct((batch, t_new, LANES), F32)],
        compiler_params=_params("parallel", "arbitrary"),
        name="sample_scores",
    )(page_table, a_rows, w_rows, ki_new_pad, *([cache_idx_kt] * pages))


def _sample_select_kernel(sp_ref, sn_ref, selp_ref, seln_ref, sc_ref, lo_ref, hi_ref, flo_ref, *, width, ktop):
    rows, past = sp_ref.shape
    n_tiles = (past + LANES) // width
    for c in range(n_tiles):
        lo, hi = c * width, (c + 1) * width
        if hi <= past:
            sc_ref[c] = sp_ref[:, lo:hi]
        elif lo == past:
            sc_ref[c] = sn_ref[...]
        else:
            sc_ref[c] = jnp.concatenate([sp_ref[:, lo:past], sn_ref[...]], axis=1)

    def get_tile(c):
        return sc_ref[c]

    def set_tile(c, val):
        sc_ref[c] = val

    thr = _topk_threshold(get_tile, set_tile, n_tiles, rows, width, ktop, lo_ref, hi_ref, flo_ref)
    for c in range(n_tiles):
        lo, hi = c * width, (c + 1) * width
        sel = jnp.concatenate([jnp.where(ch >= thr, 1.0, 0.0) for ch in _chunks(sc_ref[c])], axis=1)
        if hi <= past:
            selp_ref[:, lo:hi] = sel
        elif lo == past:
            seln_ref[...] = sel
        else:
            selp_ref[:, lo:past] = sel[:, 0:past - lo]
            seln_ref[...] = sel[:, past - lo:]


def _sample_select(sp, sn, rows, width, ktop):
    n, past = sp.shape
    spec = lambda w: pl.BlockSpec((rows, w), lambda i: (i, 0))
    return pl.pallas_call(
        functools.partial(_sample_select_kernel, width=width, ktop=ktop),
        grid=(n // rows,),
        in_specs=[spec(past), spec(LANES)],
        out_specs=[spec(past), spec(LANES)],
        out_shape=[jax.ShapeDtypeStruct((n, past), F32), jax.ShapeDtypeStruct((n, LANES), F32)],
        scratch_shapes=[pltpu.VMEM(((past + LANES) // width, rows, width), F32), pltpu.VMEM((rows, LANES), F32),
                        pltpu.VMEM((rows, LANES), F32), pltpu.VMEM((rows, LANES), F32)],
        compiler_params=_params("parallel"),
        name="sample_select",
    )(sp, sn)


def _sample_attn_kernel(pt_ref, q_ref, selp_ref, seln_ref, kn_ref, vn_ref, *rest, pages):
    kt_refs = rest[:pages]
    vt_refs = rest[pages:2 * pages]
    o_ref, m_ref, l_ref, acc_ref = rest[2 * pages:]
    step = pl.program_id(1)
    t_new = o_ref.shape[0]
    q = q_ref[...]

    def tile_rows(sel):
        return jnp.concatenate([sel] * N_HEADS_A, axis=0)

    def update(s, pv):
        m_old = m_ref[...]
        m_new = jnp.maximum(m_old, jnp.max(s, axis=1, keepdims=True))
        alpha = jnp.exp2(m_old - m_new)
        p = jnp.exp2(s - m_new)
        l_ref[...] = alpha * l_ref[...] + jnp.sum(p, axis=1, keepdims=True)
        acc_ref[...] = alpha * acc_ref[...] + pv(p.astype(BF16))
        m_ref[...] = m_new

    @pl.when(step == 0)
    def _():
        m_ref[...] = jnp.full(m_ref.shape, NEG, F32)
        l_ref[...] = jnp.zeros(l_ref.shape, F32)
        acc_ref[...] = jnp.zeros(acc_ref.shape, F32)
        s = jnp.where(tile_rows(seln_ref[...]) > 0.5, _dot_nt(q, kn_ref[...]), NEG)
        update(s, lambda p: _dot(p, vn_ref[...]))

    s = jnp.concatenate([_dot(q, kt_refs[i][...].astype(BF16)) for i in range(pages)], axis=1)
    s = jnp.where(tile_rows(selp_ref[...]) > 0.5, s, NEG)

    def pv(p):
        out = _dot_nt(p[:, 0:PAGE_SIZE], vt_refs[0][...].astype(BF16))
        for i in range(1, pages):
            out = out + _dot_nt(p[:, i * PAGE_SIZE:(i + 1) * PAGE_SIZE], vt_refs[i][...].astype(BF16))
        return out

    update(s, pv)

    @pl.when(step == pl.num_programs(1) - 1)
    def _():
        res = acc_ref[...] / l_ref[...]
        lane_head = lax.broadcasted_iota(jnp.int32, (1, D_ATTN), 1) // HEAD_DIM_A
        out = jnp.zeros((t_new, D_ATTN), F32)
        for h in range(N_HEADS_A):
            out = out + jnp.where(lane_head == h, res[h * t_new:(h + 1) * t_new], 0.0)
        o_ref[...] = out.astype(BF16)


def _sample_attn(layer, page_table, q_rows, selp, seln, kn_pad, vn_pad, cache_kt, cache_vt, n_pool, pages):
    batch, n_pages = page_table.shape
    ht = q_rows.shape[1]
    t_new = ht // N_HEADS_A
    page_spec = lambda i: pl.BlockSpec(
        (None, D_ATTN, PAGE_SIZE), lambda b, s, pt: (layer * n_pool + pt[b, s * pages + i], 0, 0))
    per_b = lambda r, w: pl.BlockSpec((None, r, w), lambda b, s, pt: (b, 0, 0))
    grid_spec = pltpu.PrefetchScalarGridSpec(
        num_scalar_prefetch=1,
        grid=(batch, n_pages // pages),
        in_specs=[per_b(ht, D_ATTN),
                  pl.BlockSpec((None, t_new, pages * PAGE_SIZE), lambda b, s, pt: (b, 0, s)),
                  per_b(t_new, LANES), per_b(LANES, D_ATTN), per_b(LANES, D_ATTN)]
        + [page_spec(i) for i in range(pages)] * 2,
        out_specs=per_b(t_new, D_ATTN),
        scratch_shapes=[pltpu.VMEM((ht, 1), F32), pltpu.VMEM((ht, 1), F32), pltpu.VMEM((ht, D_ATTN), F32)],
    )
    return pl.pallas_call(
        functools.partial(_sample_attn_kernel, pages=pages),
        grid_spec=grid_spec,
        out_shape=jax.ShapeDtypeStruct((batch, t_new, D_ATTN), BF16),
        compiler_params=_params("parallel", "arbitrary"),
        name="sample_attn",
    )(page_table, q_rows, selp, seln, kn_pad, vn_pad, *([cache_kt] * pages), *([cache_vt] * pages))


def _merge_kernel(x_ref, a_ref, g_ref, ga_ref, gb_ref, wa_ref, wb_ref, wo_ref, n2_ref, x1_ref, h2_ref, *, a_key_major):
    a_proj = _dot_tn(a_ref[...], wa_ref[...]) if a_key_major else _dot(a_ref[...], wa_ref[...])
    mixed = (ga_ref[...].astype(F32) * a_proj
             + gb_ref[...].astype(F32) * _dot(g_ref[...], wb_ref[...]))
    x1 = x_ref[...] + _dot(mixed.astype(BF16), wo_ref[...])
    x1_ref[...] = x1
    h2_ref[...] = (x1 * lax.rsqrt(jnp.mean(x1 * x1, axis=-1, keepdims=True) + EPS) * n2_ref[...]).astype(BF16)


def _merge(x2d, a_o, g_o, ga, gb, w_a, w_b, w_o, norm2, batch, tm, a_key_major):
    n, d = x2d.shape
    nt = n // batch // tm
    row = lambda w: pl.BlockSpec((tm, w), lambda b, i: (b * nt + i, 0))
    a_spec = pl.BlockSpec((None, D_ATTN, tm), lambda b, i: (b, 0, i)) if a_key_major else row(D_ATTN)
    weights = (w_a.astype(BF16), w_b.astype(BF16), w_o.astype(BF16))
    return pl.pallas_call(
        functools.partial(_merge_kernel, a_key_major=a_key_major),
        grid=(batch, nt),
        in_specs=[row(d), a_spec, row(D_GLA_V), row(d), row(d)]
        + [_const_spec(w.shape) for w in weights] + [_const_spec((1, d))],
        out_specs=[row(d), row(d)],
        out_shape=[jax.ShapeDtypeStruct((n, d), F32), jax.ShapeDtypeStruct((n, d), BF16)],
        compiler_params=_params("parallel", "parallel"),
        name="merge",
    )(x2d, a_o, g_o, ga, gb, *weights, norm2.reshape(1, d))


def _ffn_kernel(x1_ref, h2_ref, wg_ref, wu_ref, wd_ref, nf_ref, y_ref, *, n_split, final_norm):
    h2 = h2_ref[...]
    y = x1_ref[...]
    fc = wg_ref.shape[1] // n_split
    for i in range(n_split):
        g = _dot(h2, wg_ref[:, i * fc:(i + 1) * fc])
        u = _dot(h2, wu_ref[:, i * fc:(i + 1) * fc])
        act = (g * jax.nn.sigmoid(g) * u).astype(BF16)
        y = y + _dot(act, wd_ref[i * fc:(i + 1) * fc, :])
    if final_norm:
        y = y * lax.rsqrt(jnp.mean(y * y, axis=-1, keepdims=True) + EPS) * nf_ref[...]
    y_ref[...] = y


def _ffn(x1, h2, w_g, w_u, w_d, norm_final, tm, final_norm):
    n, d = x1.shape
    d_ff = w_g.shape[1]
    n_split = 2 if d_ff % (2 * LANES) == 0 else 1
    row = lambda: pl.BlockSpec((tm, d), lambda i: (i, 0))
    weights = (w_g.astype(BF16), w_u.astype(BF16), w_d.astype(BF16))
    return pl.pallas_call(
        functools.partial(_ffn_kernel, n_split=n_split, final_norm=final_norm),
        grid=(n // tm,),
        in_specs=[row(), row()] + [_const_spec(w.shape) for w in weights] + [_const_spec((1, d))],
        out_specs=row(),
        out_shape=jax.ShapeDtypeStruct((n, d), F32),
        compiler_params=_params("parallel"),
        name="ffn",
    )(x1, h2, *weights, norm_final.reshape(1, d))


def _pick(n, prefs):
    for p in prefs:
        if n % p == 0:
            return p
    return n


def kernel(x_prompt, x_sample, cache_k, cache_v, cache_idx_k, state_gla, page_table, norm1, w_in, w_gate_up,
           b_gate, g_gla_norm, w_branch_a, w_branch_b, w_out, norm2, w_ffn_gate, w_ffn_up, w_ffn_down, norm_final):
    depth = w_in.shape[0]
    bp, seq, d = x_prompt.shape
    bs, t_new, _ = x_sample.shape
    n_pool = cache_k.shape[1]
    n_pages = page_table.shape[1]
    past = n_pages * PAGE_SIZE
    xp = x_prompt.reshape(bp * seq, d)
    xs = x_sample.reshape(bs * t_new, d)
    cache_kt = cache_k.transpose(0, 1, 3, 4, 2).reshape(depth * n_pool, D_ATTN, PAGE_SIZE)
    cache_vt = cache_v.transpose(0, 1, 3, 4, 2).reshape(depth * n_pool, D_ATTN, PAGE_SIZE)
    cache_ikt = cache_idx_k.transpose(0, 1, 3, 2).reshape(depth * n_pool, D_IDX, PAGE_SIZE)
    outs = [[] for _ in range(8)]
    tq = _pick(seq, (256, 128))
    for l in range(depth):
        last = l == depth - 1
        (qkg, vg, la, r, ga, gb, qt, qit, wt, kb, kk, kt, vt, vtb, kit) = _inproj(
            xp, norm1[l], w_in[l], w_gate_up[l], b_gate[l], bp, _pick(seq, (256, 128)), True)
        a_o = _dsa_prompt(qt, qit, wt, kb, vtb, kk, bp, seq, tq)
        chunk = GLA_CHUNK if seq % GLA_CHUNK == 0 else seq
        g_o, st = _gla(qkg, vg, la, r, g_gla_norm[l], jnp.zeros((bp, DV_G, D_GLA_K), F32), bp, seq, chunk,
                       _pick(seq, (512, 256, 128, 64)), _pick(bp, (2, 1)))
        x1, h2 = _merge(xp, a_o, g_o, ga, gb, w_branch_a[l], w_branch_b[l], w_out[l], norm2[l], bp,
                        _pick(seq, (512, 256, 128)), True)
        xp = _ffn(x1, h2, w_ffn_gate[l], w_ffn_up[l], w_ffn_down[l], norm_final, _pick(bp * seq, (256, 128)), last)
        outs[0].append(kt.reshape(bp, N_HEADS_A, HEAD_DIM_A, seq).transpose(0, 3, 1, 2))
        outs[1].append(vt.reshape(bp, N_HEADS_A, HEAD_DIM_A, seq).transpose(0, 3, 1, 2))
        outs[2].append(kit.transpose(0, 2, 1))
        outs[3].append(_state_from_t(st))

        (qkg, vg, la, r, ga, gb, q, qi, kw, k, v, kb, vb, kk) = _inproj(
            xs, norm1[l], w_in[l], w_gate_up[l], b_gate[l], 1, _pick(bs * t_new, (256, 128)), False)
        a_rows = qi.reshape(bs, t_new, N_HEADS_IDX, D_IDX).transpose(0, 2, 1, 3).reshape(bs, N_HEADS_IDX * t_new, D_IDX)
        w_rows = (kw[:, D_IDX:D_IDX + N_HEADS_IDX] * (N_HEADS_IDX ** -0.5 * D_IDX ** -0.5)).reshape(
            bs, t_new, N_HEADS_IDX).transpose(0, 2, 1).reshape(bs, N_HEADS_IDX * t_new, 1)
        w_rows = jnp.broadcast_to(w_rows, (bs, N_HEADS_IDX * t_new, LANES))
        pad_rows = lambda a: jnp.pad(a.reshape(bs, t_new, a.shape[-1]), ((0, 0), (0, LANES - t_new), (0, 0)))
        sp, sn = _sample_scores(l, page_table, a_rows, w_rows, pad_rows(kk[:, 0:D_IDX]), cache_ikt, n_pool,
                                _pick(n_pages, (32, 16, 8, 4, 2, 1)))
        ktop = min(TOPK_MAX, (past + t_new) // 4)
        width = 5 * LANES if (past + LANES) % (5 * LANES) == 0 else LANES
        rows = _pick(bs * t_new, (128, 64, 32, 16, 8))
        selp, seln = _sample_select(sp.reshape(bs * t_new, past), sn.reshape(bs * t_new, LANES), rows, width, ktop)
        head_of_lane = jnp.arange(D_ATTN) // HEAD_DIM_A
        q_rows = jnp.where(head_of_lane[None, None, None, :] == jnp.arange(N_HEADS_A)[None, :, None, None],
                           q.reshape(bs, 1, t_new, D_ATTN), jnp.zeros((), BF16)).reshape(bs, N_HEADS_A * t_new, D_ATTN)
        a_o = _sample_attn(l, page_table, q_rows, selp.reshape(bs, t_new, past), seln.reshape(bs, t_new, LANES),
                           pad_rows(kb), pad_rows(vb), cache_kt, cache_vt, n_pool, _pick(n_pages, (16, 8, 4, 2, 1)))
        gchunk = 16
        pad_g = lambda a: jnp.pad(a.reshape(bs, t_new, a.shape[-1]), ((0, 0), (0, gchunk - t_new), (0, 0))).reshape(
            bs * gchunk, a.shape[-1])
        g_o, st = _gla(pad_g(qkg), pad_g(vg), pad_g(la), pad_g(r), g_gla_norm[l], _state_to_t(state_gla[l].astype(F32)),
                       bs, gchunk, gchunk, gchunk, _pick(bs, (8, 4, 2, 1)))
        g_o = g_o.reshape(bs, gchunk, D_GLA_V)[:, 0:t_new].reshape(bs * t_new, D_GLA_V)
        x1, h2 = _merge(xs, a_o.reshape(bs * t_new, D_ATTN), g_o, ga, gb, w_branch_a[l], w_branch_b[l], w_out[l],
                        norm2[l], 1, _pick(bs * t_new, (512, 256, 128)), False)
        xs = _ffn(x1, h2, w_ffn_gate[l], w_ffn_up[l], w_ffn_down[l], norm_final, _pick(bs * t_new, (256, 128)), last)
        outs[4].append(k.reshape(bs, t_new, N_HEADS_A, HEAD_DIM_A))
        outs[5].append(v.reshape(bs, t_new, N_HEADS_A, HEAD_DIM_A))
        outs[6].append(kw[:, 0:D_IDX].reshape(bs, t_new, D_IDX))
        outs[7].append(_state_from_t(st))
    return (xp.reshape(bp, seq, d), xs.reshape(bs, t_new, d)) + tuple(jnp.stack(o) for o in outs)
```

```python
import functools

import jax
import jax.numpy as jnp
from jax import lax
from jax.experimental import pallas as pl
from jax.experimental.pallas import tpu as pltpu

F32 = jnp.float32
BF16 = jnp.bfloat16

D_ATTN = 512
N_HEADS_A = 8
HEAD_DIM_A = 64
N_HEADS_IDX = 8
D_IDX = 64
TOPK_MAX = 256
N_HEADS_G = 4
DK_G = 64
DV_G = 128
D_GLA_K = N_HEADS_G * DK_G
D_GLA_V = N_HEADS_G * DV_G
GATE_RANK = 16
GATE_TAU = 16.0
GLA_CHUNK = 64
EPS = 1e-6
PAGE_SIZE = 128

LANES = 128
NEG = -1e30
LOWEST = -3.0e38
MAX_BISECT = 64
Q_SCALE = HEAD_DIM_A ** -0.5 * 1.4426950408889634
VMEM_LIMIT = 56 * 1024 * 1024


def _dot(a, b):
    return jnp.dot(a, b, preferred_element_type=F32)


def _dot_nt(a, b):
    return lax.dot_general(a, b, (((1,), (1,)), ((), ())), preferred_element_type=F32)


def _dot_tn(a, b):
    return lax.dot_general(a, b, (((0,), (0,)), ((), ())), preferred_element_type=F32)


def _params(*sem):
    return pltpu.CompilerParams(dimension_semantics=sem, vmem_limit_bytes=VMEM_LIMIT)


def _const_spec(shape):
    n = len(shape)
    return pl.BlockSpec(shape, lambda *_: (0,) * n)


def _chunks(x):
    return [x[:, i * LANES:(i + 1) * LANES] for i in range(x.shape[1] // LANES)]


def _rep(col, rows):
    return jnp.broadcast_to(col, (rows, LANES))


def _inproj_kernel(x_ref, g_ref, wc_ref, wd_ref, we_ref, wgu_ref, bg_ref, *refs, key_major):
    x = x_ref[...]
    h = (x * lax.rsqrt(jnp.mean(x * x, axis=-1, keepdims=True) + EPS) * g_ref[...]).astype(BF16)
    if key_major:
        (wqt_ref, wkvt_ref, wkk_ref, wkit_ref, wwit_ref,
         qkg_ref, vg_ref, la_ref, r_ref, ga_ref, gb_ref,
         qt_ref, qit_ref, wt_ref, kb_ref, kk_ref, kt_ref, vt_ref, vtb_ref, kit_ref) = refs
        qq = _dot_nt(wqt_ref[...], h)
        qt_ref[...] = (qq[0:512] * Q_SCALE).astype(BF16)
        qit_ref[...] = qq[512:1024].astype(BF16)
        kv = _dot_nt(wkvt_ref[...], h)
        kt_ref[...] = kv[0:512]
        vt_ref[...] = kv[512:1024]
        vtb_ref[...] = kv[512:1024].astype(BF16)
        kb_ref[...] = kv[0:512].T.astype(BF16)
        kk_ref[...] = _dot(h, wkk_ref[...]).astype(BF16)
        kit_ref[...] = _dot_nt(wkit_ref[...], h)
        wt_ref[...] = _dot_nt(wwit_ref[...], h)[0:N_HEADS_IDX] * (N_HEADS_IDX ** -0.5 * D_IDX ** -0.5)
    else:
        (wq_ref, wkv_ref, wkk_ref, wkw_ref,
         qkg_ref, vg_ref, la_ref, r_ref, ga_ref, gb_ref,
         q_ref, qi_ref, kw_ref, k_ref, v_ref, kb_ref, vb_ref, kk_ref) = refs
        a = _dot(h, wq_ref[...])
        q_ref[...] = (a[:, 0:512] * Q_SCALE).astype(BF16)
        qi_ref[...] = a[:, 512:1024].astype(BF16)
        kw_ref[...] = _dot(h, wkw_ref[...])
        kv = _dot(h, wkv_ref[...])
        k_ref[...] = kv[:, 0:512]
        v_ref[...] = kv[:, 512:1024]
        kb_ref[...] = kv[:, 0:512].astype(BF16)
        vb_ref[...] = kv[:, 512:1024].astype(BF16)
        kk_ref[...] = _dot(h, wkk_ref[...]).astype(BF16)
    c = _dot(h, wc_ref[...])
    qkg_ref[...] = c[:, 0:512]
    vg_ref[...] = c[:, 512:1024]
    glr = _dot(h, wd_ref[...]).astype(BF16)
    z = _dot(glr, wgu_ref[...]) + bg_ref[...]
    la_ref[...] = (jnp.minimum(z, 0.0) - jnp.log1p(jnp.exp(-jnp.abs(z)))) * (1.0 / GATE_TAU)
    e = _dot(h, we_ref[...])
    r_ref[...] = e[:, 0:512]
    ga_ref[...] = jax.nn.sigmoid(e[:, 512:1536]).astype(BF16)
    gb_ref[...] = jax.nn.sigmoid(e[:, 1536:2560]).astype(BF16)


def _inproj(x2d, norm_g, w_in, w_gate_up, b_gate, batch, tm, key_major):
    n, d = x2d.shape
    t_len = n // batch
    nt = t_len // tm
    wb = w_in.astype(BF16)
    wq = jnp.concatenate([wb[:, 0:512], wb[:, 1536:2048]], axis=1)
    wkv = wb[:, 512:1536]
    wki = wb[:, 2048:2112]
    wwi = wb[:, 2112:2120]
    wkk = jnp.concatenate([wki, wki], axis=1)
    wc = wb[:, 2120:3144]
    wd = jnp.concatenate([wb[:, 3144:3160], jnp.zeros((d, LANES - GATE_RANK), BF16)], axis=1)
    we = wb[:, 3160:5720]
    wgu = jnp.concatenate([w_gate_up.astype(BF16), jnp.zeros((LANES - GATE_RANK, D_GLA_K), BF16)], axis=0)
    row = lambda w: pl.BlockSpec((tm, w), lambda b, i: (b * nt + i, 0))
    col = lambda r: pl.BlockSpec((None, r, tm), lambda b, i: (b, 0, i))
    rows_out = lambda outs: ([row(w) for w, _ in outs], [jax.ShapeDtypeStruct((n, w), dt) for w, dt in outs])
    cols_out = lambda outs: ([col(r) for r, _ in outs],
                             [jax.ShapeDtypeStruct((batch, r, t_len), dt) for r, dt in outs])
    out_specs, out_shape = rows_out([(512, F32), (512, F32), (256, F32), (512, F32), (1024, BF16), (1024, BF16)])
    if key_major:
        wwit = jnp.concatenate([wwi.T, jnp.zeros((16 - N_HEADS_IDX, d), BF16)], axis=0)
        flavor_w = (wq.T, wkv.T, wkk, wki.T, wwit)
        s1, h1 = cols_out([(512, BF16), (512, BF16), (N_HEADS_IDX, F32)])
        s2, h2 = rows_out([(512, BF16), (LANES, BF16)])
        s3, h3 = cols_out([(512, F32), (512, F32), (512, BF16), (D_IDX, F32)])
        out_specs, out_shape = out_specs + s1 + s2 + s3, out_shape + h1 + h2 + h3
    else:
        wkw = jnp.concatenate([wki, wwi, jnp.zeros((d, LANES - 72), BF16)], axis=1)
        flavor_w = (wq, wkv, wkk, wkw)
        s1, h1 = rows_out([(512, BF16), (512, BF16), (LANES, F32), (512, F32), (512, F32), (512, BF16),
                           (512, BF16), (LANES, BF16)])
        out_specs, out_shape = out_specs + s1, out_shape + h1
    weights = (wc, wd, we, wgu)
    return pl.pallas_call(
        functools.partial(_inproj_kernel, key_major=key_major),
        grid=(batch, nt),
        in_specs=[row(d), _const_spec((1, d))] + [_const_spec(w.shape) for w in weights]
        + [_const_spec((1, D_GLA_K))] + [_const_spec(w.shape) for w in flavor_w],
        out_specs=out_specs,
        out_shape=out_shape,
        compiler_params=_params("parallel", "parallel"),
        name="inproj",
    )(x2d, norm_g.reshape(1, d), *weights, b_gate.reshape(1, D_GLA_K), *flavor_w)


def _gla_kernel(qk_ref, v_ref, la_ref, r_ref, gn_ref, s0_ref, go_ref, sfin_ref, st_ref, *, chunk, n_chunks):
    step = pl.program_id(1)

    @pl.when(step == 0)
    def _():
        st_ref[...] = s0_ref[...]

    ri = lax.broadcasted_iota(jnp.int32, (chunk, chunk), 0)
    ci = lax.broadcasted_iota(jnp.int32, (chunk, chunk), 1)
    tril = ci <= ri
    tri = jnp.where(tril, 1.0, 0.0).astype(F32)
    lane_head = lax.broadcasted_iota(jnp.int32, (1, D_GLA_K), 1) // DK_G
    mid = chunk // 2 - 1
    gn = gn_ref[...]

    def body(c, carry):
        for g in range(st_ref.shape[0]):
            one_chunk(c, g)
        return carry

    def one_chunk(c, g):
        r0 = pl.multiple_of(c * chunk, chunk)
        qk = qk_ref[g, pl.ds(r0, chunk), :]
        q = qk[:, 0:D_GLA_K] * DK_G ** -0.5
        k = qk[:, D_GLA_K:2 * D_GLA_K]
        la = la_ref[g, pl.ds(r0, chunk), :]
        b = jnp.dot(tri, la, precision=lax.Precision.HIGHEST, preferred_element_type=F32)
        b_mid = b[mid:mid + 1, :]
        b_end = b[chunk - 1:chunk, :]
        qe = q * jnp.exp(b)
        qt = q * jnp.exp(b - b_mid)
        ktb = (k * jnp.exp(b_mid - b)).astype(BF16)
        kdb = (k * jnp.exp(b_end - b)).astype(BF16)
        st = st_ref[g]
        stb = st.astype(BF16)
        new_st = st * jnp.exp(b_end)
        for h in range(N_HEADS_G):
            hm = lane_head == h
            qt_h = jnp.where(hm, qt, 0.0).astype(BF16)
            qe_h = jnp.where(hm, qe, 0.0).astype(BF16)
            att = jnp.where(tril, _dot_nt(qt_h, ktb), 0.0)
            vb = v_ref[g, pl.ds(r0, chunk), h * DV_G:(h + 1) * DV_G].astype(BF16)
            o = _dot_nt(qe_h, stb) + _dot(att.astype(BF16), vb)
            new_st = new_st + jnp.where(hm, _dot_tn(vb, kdb), 0.0)
            on = o * lax.rsqrt(jnp.mean(o * o, axis=-1, keepdims=True) + EPS) * gn
            rr = r_ref[g, pl.ds(r0, chunk), h * DV_G:(h + 1) * DV_G]
            go_ref[g, pl.ds(r0, chunk), h * DV_G:(h + 1) * DV_G] = (on * (rr * jax.nn.sigmoid(rr))).astype(BF16)
        st_ref[g] = new_st

    lax.fori_loop(0, n_chunks, body, 0)

    @pl.when(step == pl.num_programs(1) - 1)
    def _():
        sfin_ref[...] = st_ref[...]


def _gla(qkg, vg, la, r, g_norm, s0_t, batch, t_len, chunk, rows_per_step, group):
    n_steps = t_len // rows_per_step
    row = lambda w: pl.BlockSpec((group, rows_per_step, w), lambda b, s: (b, s, 0))
    st_spec = pl.BlockSpec((group, DV_G, D_GLA_K), lambda b, s: (b, 0, 0))
    seqs = lambda a: a.reshape(batch, t_len, a.shape[-1])
    g_o, st = pl.pallas_call(
        functools.partial(_gla_kernel, chunk=chunk, n_chunks=rows_per_step // chunk),
        grid=(batch // group, n_steps),
        in_specs=[row(512), row(512), row(256), row(512), _const_spec((1, DV_G)), st_spec],
        out_specs=[row(512), st_spec],
        out_shape=[jax.ShapeDtypeStruct((batch, t_len, D_GLA_V), BF16),
                   jax.ShapeDtypeStruct((batch, DV_G, D_GLA_K), F32)],
        scratch_shapes=[pltpu.VMEM((group, DV_G, D_GLA_K), F32)],
        compiler_params=_params("parallel", "arbitrary"),
        name="gla",
    )(seqs(qkg), seqs(vg), seqs(la), seqs(r), g_norm.reshape(1, DV_G), s0_t)
    return g_o.reshape(batch * t_len, D_GLA_V), st


def _state_to_t(s):
    b = s.shape[0]
    return s.transpose(0, 3, 1, 2).reshape(b, DV_G, D_GLA_K)


def _state_from_t(st):
    b = st.shape[0]
    return st.reshape(b, DV_G, N_HEADS_G, DK_G).transpose(0, 2, 3, 1)


def _topk_threshold(get_tile, set_tile, n_tiles, rows, width, ktop, lo_ref, hi_ref, flo_ref):
    kf = float(ktop)
    one = lambda m: jnp.where(m, 1.0, 0.0)
    zeros = jnp.zeros((rows, LANES), F32)
    rsum = lambda a: _rep(jnp.sum(a, axis=1, keepdims=True), rows)

    def acc_chunks(blk, acc, fn, op):
        for ch in _chunks(blk):
            acc = op(acc, fn(ch))
        return acc

    def stats(c, st):
        mx, mn, nv, c0, c0e = st
        blk = get_tile(c)
        mx = acc_chunks(blk, mx, lambda ch: ch, jnp.maximum)
        mn = acc_chunks(blk, mn, lambda ch: jnp.where(ch > -jnp.inf, ch, jnp.inf), jnp.minimum)
        nv = acc_chunks(blk, nv, lambda ch: one(ch > -jnp.inf), jnp.add)
        c0 = acc_chunks(blk, c0, lambda ch: one(ch > 0.0), jnp.add)
        c0e = acc_chunks(blk, c0e, lambda ch: one(ch >= 0.0), jnp.add)
        return mx, mn, nv, c0, c0e

    mx, mn, nv, c0, c0e = lax.fori_loop(
        0, n_tiles, stats,
        (jnp.full((rows, LANES), -jnp.inf, F32), jnp.full((rows, LANES), jnp.inf, F32), zeros, zeros, zeros))
    rowmax = _rep(jnp.max(mx, axis=1, keepdims=True), rows)
    rowmin = _rep(jnp.min(mn, axis=1, keepdims=True), rows)
    nvalid, cnt0, cnt0e = rsum(nv), rsum(c0), rsum(c0e)

    def count_gt(thr):
        def body(c, acc):
            return acc_chunks(get_tile(c), acc, lambda ch: one(ch > thr), jnp.add)
        return rsum(lax.fori_loop(0, n_tiles, body, zeros))

    small = nvalid <= kf
    ztie = jnp.logical_and(jnp.logical_and(cnt0 < kf, cnt0e >= kf), jnp.logical_not(small))
    frozen = jnp.logical_or(small, ztie)
    pos = cnt0 >= kf
    lo_ref[...] = jnp.where(frozen, 0.0, jnp.where(pos, 0.0, jnp.minimum(rowmin - 1.0, rowmin * 2.0)))
    hi_ref[...] = jnp.where(frozen, 0.0, jnp.where(pos, rowmax, 0.0))
    flo_ref[...] = jnp.where(frozen, kf, jnp.where(pos, cnt0, nvalid))

    def cond(st):
        it, pending = st
        return jnp.logical_and(it < MAX_BISECT, pending > 0)

    def bisect(st):
        it, _ = st
        lo, hi = lo_ref[...], hi_ref[...]
        mid = 0.5 * (lo + hi)
        c = count_gt(mid)
        ge = c >= kf
        flo = jnp.where(ge, c, flo_ref[...])
        lo_ref[...] = jnp.where(ge, mid, lo)
        hi_ref[...] = jnp.where(ge, hi, mid)
        flo_ref[...] = flo
        return it + 1, jnp.max(one(flo != kf)).astype(jnp.int32)

    pending0 = jnp.max(one(flo_ref[...] != kf)).astype(jnp.int32)
    lax.while_loop(cond, bisect, (jnp.int32(0), pending0))

    lo = lo_ref[...]

    def tmin(c, acc):
        return acc_chunks(get_tile(c), acc, lambda ch: jnp.where(ch > lo, ch, jnp.inf), jnp.minimum)

    thr = _rep(jnp.min(lax.fori_loop(0, n_tiles, tmin, jnp.full((rows, LANES), jnp.inf, F32)),
                       axis=1, keepdims=True), rows)
    thr = jnp.where(ztie, 0.0, thr)
    thr = jnp.where(small, LOWEST, thr)

    def gteq(c, st):
        g, e = st
        blk = get_tile(c)
        return (acc_chunks(blk, g, lambda ch: one(ch > thr), jnp.add),
                acc_chunks(blk, e, lambda ch: one(ch == thr), jnp.add))

    g, e = lax.fori_loop(0, n_tiles, gteq, (zeros, zeros))
    need = kf - rsum(g)
    any_tie = jnp.max(one(rsum(e) > need)) > 0.0

    @pl.when(any_tie)
    def _():
        ri = lax.broadcasted_iota(jnp.int32, (width, width), 0)
        ci = lax.broadcasted_iota(jnp.int32, (width, width), 1)
        upper = jnp.where(ri <= ci, 1.0, 0.0).astype(BF16)
        thr_c, need_c = thr[:, 0:1], need[:, 0:1]

        def drop(c, seen):
            blk = get_tile(c)
            eq = blk == thr_c
            eqf = one(eq)
            rank = _dot(eqf.astype(BF16), upper) + seen - 1.0
            set_tile(c, jnp.where(jnp.logical_and(eq, rank >= need_c), -jnp.inf, blk))
            return seen + jnp.sum(eqf, axis=1, keepdims=True)

        lax.fori_loop(0, n_tiles, drop, jnp.zeros((rows, 1), F32))

    return thr


SUB = 8


def _groups(x):
    return [x[i * SUB:(i + 1) * SUB] for i in range(x.shape[0] // SUB)]


def _tree(op, xs):
    xs = list(xs)
    while len(xs) > 1:
        xs = [op(xs[i], xs[i + 1]) if i + 1 < len(xs) else xs[i] for i in range(0, len(xs), 2)]
    return xs[0]


def _col(op, x8):
    return jnp.broadcast_to(op(x8, axis=0, keepdims=True), x8.shape)


def _col_stats_init(n):
    zeros = jnp.zeros((SUB, n), F32)
    return jnp.full((SUB, n), -jnp.inf, F32), jnp.full((SUB, n), jnp.inf, F32), zeros, zeros, zeros


def _col_stats_update(st, tile):
    one = lambda m: jnp.where(m, 1.0, 0.0)
    mx, mn, nv, c0, c0e = st
    g = _groups(tile)
    mx = jnp.maximum(mx, _tree(jnp.maximum, g))
    mn = jnp.minimum(mn, _tree(jnp.minimum, [jnp.where(x > -jnp.inf, x, jnp.inf) for x in g]))
    nv = nv + _tree(jnp.add, [one(x > -jnp.inf) for x in g])
    c0 = c0 + _tree(jnp.add, [one(x > 0.0) for x in g])
    c0e = c0e + _tree(jnp.add, [one(x >= 0.0) for x in g])
    return mx, mn, nv, c0, c0e


def _topk_cut_cols(get_tile, set_tile, n_tiles, tk, n, ktop, stats, lo_ref, hi_ref, flo_ref):
    kf = float(ktop)
    one = lambda m: jnp.where(m, 1.0, 0.0)
    zeros = jnp.zeros((SUB, n), F32)
    total = lambda a: _col(jnp.sum, a)
    mx, mn, nv, c0, c0e = stats
    colmax, colmin = _col(jnp.max, mx), _col(jnp.min, mn)
    nvalid, cnt0, cnt0e = total(nv), total(c0), total(c0e)

    def count_gt(thr):
        def body(c, acc):
            return acc + _tree(jnp.add, [one(x > thr) for x in _groups(get_tile(c))])
        return total(lax.fori_loop(0, n_tiles, body, zeros))

    small = nvalid <= kf
    ztie = jnp.logical_and(jnp.logical_and(cnt0 < kf, cnt0e >= kf), jnp.logical_not(small))
    frozen = jnp.logical_or(small, ztie)
    pos = cnt0 >= kf
    lo_ref[...] = jnp.where(frozen, 0.0, jnp.where(pos, 0.0, jnp.minimum(colmin - 1.0, colmin * 2.0)))
    hi_ref[...] = jnp.where(frozen, 0.0, jnp.where(pos, colmax, 0.0))
    flo_ref[...] = jnp.where(frozen, kf, jnp.where(pos, cnt0, nvalid))

    def cond(st):
        it, pending = st
        return jnp.logical_and(it < MAX_BISECT, pending > 0)

    def bisect_until(slack):
        def bisect(st):
            it, _ = st
            lo, hi = lo_ref[...], hi_ref[...]
            mid = 0.5 * (lo + hi)
            c = count_gt(mid)
            ge = c >= kf
            flo = jnp.where(ge, c, flo_ref[...])
            lo_ref[...] = jnp.where(ge, mid, lo)
            hi_ref[...] = jnp.where(ge, hi, mid)
            flo_ref[...] = flo
            return it + 1, jnp.max(one(flo > kf + slack)).astype(jnp.int32)
        return bisect

    def next_above(lo):
        def body(c, acc):
            return jnp.minimum(
                acc, _tree(jnp.minimum, [jnp.where(x > lo, x, jnp.inf) for x in _groups(get_tile(c))]))
        return _col(jnp.min, lax.fori_loop(0, n_tiles, body, jnp.full((SUB, n), jnp.inf, F32)))

    pending0 = jnp.max(one(flo_ref[...] > kf + 1.0)).astype(jnp.int32)
    it1, _ = lax.while_loop(cond, bisect_until(1.0), (jnp.int32(0), pending0))
    lo1, flo1 = lo_ref[...], flo_ref[...]
    lo_adv = jnp.where(flo1 == kf + 1.0, next_above(lo1), lo1)
    stepped = count_gt(lo_adv) == kf
    lo_ref[...] = jnp.where(stepped, lo_adv, lo1)
    flo_ref[...] = jnp.where(stepped, kf, flo1)
    pending1 = jnp.max(one(flo_ref[...] != kf)).astype(jnp.int32)
    lax.while_loop(cond, bisect_until(0.0), (it1, pending1))

    lo = lo_ref[...]
    unresolved = jnp.logical_or(ztie, flo_ref[...] != kf)
    lo_ref[...] = jnp.where(small, -jnp.inf, lo)

    @pl.when(jnp.max(one(unresolved)) > 0.0)
    def _():
        thr = jnp.where(ztie, 0.0, next_above(lo))

        def gt(c, g):
            return g + _tree(jnp.add, [one(x > thr) for x in _groups(get_tile(c))])

        need = kf - total(lax.fori_loop(0, n_tiles, gt, zeros))
        ri = lax.broadcasted_iota(jnp.int32, (tk, tk), 0)
        ci = lax.broadcasted_iota(jnp.int32, (tk, tk), 1)
        lower = jnp.where(ci <= ri, 1.0, 0.0).astype(BF16)
        thr_r, need_r, unres_r = thr[0:1], need[0:1], unresolved[0:1]

        def drop(c, st):
            seen, below = st
            blk = get_tile(c)
            eq = jnp.logical_and(blk == thr_r, unres_r)
            eqf = one(eq)
            rank = _dot(lower, eqf.astype(BF16)) + seen - 1.0
            set_tile(c, jnp.where(jnp.logical_and(eq, rank >= need_r), -jnp.inf, blk))
            below = jnp.maximum(below, jnp.max(jnp.where(blk < thr_r, blk, -jnp.inf), axis=0, keepdims=True))
            return seen + jnp.sum(eqf, axis=0, keepdims=True), below

        _, below = lax.fori_loop(0, n_tiles, drop, (jnp.zeros((1, n), F32), jnp.full((1, n), -jnp.inf, F32)))
        lo_ref[...] = jnp.where(unresolved, jnp.broadcast_to(below, (SUB, n)), lo_ref[...])

    return lo_ref[...]


def _dsa_prompt_kernel(qt_ref, qit_ref, wt_ref, k_ref, vt_ref, kk_ref, o_ref,
                       sc_ref, lo_ref, hi_ref, flo_ref, m_ref, l_ref, acc_ref, sa_ref, sb_ref, *, tq, ktop):
    j = pl.program_id(1)
    n_tiles = j + 1
    t0 = j * tq
    qpos = t0 + lax.broadcasted_iota(jnp.int32, (1, tq), 1)
    kiota = lax.broadcasted_iota(jnp.int32, (tq, 1), 0)
    per = LANES // HEAD_DIM_A
    reps = tq // SUB
    zero_rows = jnp.zeros((LANES - D_IDX, tq), BF16)

    def tall(x8):
        return pltpu.repeat(x8, reps, axis=0)

    qit = qit_ref[...]
    wt = wt_ref[...]
    qih = [jnp.concatenate([qit[h * D_IDX:(h + 1) * D_IDX], zero_rows], axis=0) for h in range(N_HEADS_IDX)]
    wrow = [tall(jnp.broadcast_to(wt[h:h + 1], (SUB, tq))) for h in range(N_HEADS_IDX)]

    def score_tile(c, stats):
        k0 = pl.multiple_of(c * tq, tq)
        kk = kk_ref[pl.ds(k0, tq), :]
        acc = jnp.zeros((tq, tq), F32)
        for h in range(N_HEADS_IDX):
            acc = acc + jnp.maximum(_dot(kk, qih[h]), 0.0) * wrow[h]
        tile = jnp.where(k0 + kiota <= qpos, acc, -jnp.inf)
        sc_ref[c] = tile
        return _col_stats_update(stats, tile)

    stats = lax.fori_loop(0, n_tiles, score_tile, _col_stats_init(tq))

    def get_tile(c):
        return sc_ref[c]

    def set_tile(c, val):
        sc_ref[c] = val

    cut = tall(_topk_cut_cols(get_tile, set_tile, n_tiles, tq, tq, ktop, stats, lo_ref, hi_ref, flo_ref))

    qt = qt_ref[...]
    zq = jnp.zeros((HEAD_DIM_A, tq), BF16)
    qh = []
    for h in range(N_HEADS_A):
        rows = qt[h * HEAD_DIM_A:(h + 1) * HEAD_DIM_A]
        qh.append(jnp.concatenate([zq] * (h % per) + [rows] + [zq] * (per - 1 - h % per), axis=0))
    m_ref[...] = jnp.full(m_ref.shape, NEG, F32)
    l_ref[...] = jnp.zeros(l_ref.shape, F32)
    acc_ref[...] = jnp.zeros(acc_ref.shape, F32)

    ta = tq // 2
    cut_a = cut[0:ta]
    ones_rows = jnp.ones((2 * SUB, ta), BF16)
    last = 2 * n_tiles - 1

    def logits(c, buf):
        k0 = pl.multiple_of(c * ta, ta)
        for h in range(N_HEADS_A):
            buf[h] = _dot(k_ref[pl.ds(k0, ta), (h // per) * LANES:(h // per + 1) * LANES], qh[h])

    def softmax_pv(c, half, buf):
        k0 = pl.multiple_of(c * ta, ta)
        bias = jnp.where(sc_ref[c // 2, half * ta:(half + 1) * ta, :] > cut_a, 0.0, NEG)
        for h in range(N_HEADS_A):
            s = buf[h] + bias
            vh = jnp.concatenate([vt_ref[h * HEAD_DIM_A:(h + 1) * HEAD_DIM_A, pl.ds(k0, ta)], ones_rows], axis=0)
            m_old = m_ref[h]
            m_new = jnp.maximum(m_old, _col(jnp.max, _tree(jnp.maximum, _groups(s))))
            alpha = jnp.exp2(m_old - m_new)
            p = jnp.exp2(s - pltpu.repeat(m_new, ta // SUB, axis=0))
            pv = _dot(vh, p.astype(BF16))
            l_ref[h] = alpha * l_ref[h] + pv[HEAD_DIM_A:HEAD_DIM_A + SUB]
            acc_ref[h] = pltpu.repeat(alpha, HEAD_DIM_A // SUB, axis=0) * acc_ref[h] + pv[0:HEAD_DIM_A]
            m_ref[h] = m_new

    logits(0, sa_ref)

    def att_tile(i, carry):
        logits(2 * i + 1, sb_ref)
        softmax_pv(2 * i, 0, sa_ref)
        logits(jnp.minimum(2 * i + 2, last), sa_ref)
        softmax_pv(2 * i + 1, 1, sb_ref)
        return carry

    lax.fori_loop(0, n_tiles, att_tile, 0)

    for h in range(N_HEADS_A):
        o_ref[h * HEAD_DIM_A:(h + 1) * HEAD_DIM_A, :] = (
            acc_ref[h] / pltpu.repeat(l_ref[h], HEAD_DIM_A // SUB, axis=0)).astype(BF16)


def _dsa_prompt(qt, qit, wt, kb, vtb, kk, batch, seq, tq):
    nq = seq // tq
    ktop = min(TOPK_MAX, seq // 4)
    qcol = lambda r: pl.BlockSpec((None, r, tq), lambda b, j: (b, 0, j))
    rows = lambda w: pl.BlockSpec((seq, w), lambda b, j: (b, 0))
    return pl.pallas_call(
        functools.partial(_dsa_prompt_kernel, tq=tq, ktop=ktop),
        grid=(batch, nq),
        in_specs=[qcol(512), qcol(512), qcol(N_HEADS_IDX), rows(512),
                  pl.BlockSpec((None, 512, seq), lambda b, j: (b, 0, 0)), rows(LANES)],
        out_specs=qcol(512),
        out_shape=jax.ShapeDtypeStruct((batch, D_ATTN, seq), BF16),
        scratch_shapes=[pltpu.VMEM((nq, tq, tq), F32), pltpu.VMEM((SUB, tq), F32), pltpu.VMEM((SUB, tq), F32),
                        pltpu.VMEM((SUB, tq), F32), pltpu.VMEM((N_HEADS_A, SUB, tq), F32),
                        pltpu.VMEM((N_HEADS_A, SUB, tq), F32), pltpu.VMEM((N_HEADS_A, HEAD_DIM_A, tq), F32),
                        pltpu.VMEM((N_HEADS_A, tq // 2, tq), F32), pltpu.VMEM((N_HEADS_A, tq // 2, tq), F32)],
        compiler_params=_params("parallel", "arbitrary"),
        name="dsa_prompt",
    )(qt, qit, wt, kb, vtb, kk)


def _sample_scores_kernel(pt_ref, a_ref, w_ref, kin_ref, *rest, pages):
    page_refs = rest[:pages]
    sp_ref, sn_ref = rest[pages], rest[pages + 1]
    t_new = sn_ref.shape[0]
    a = a_ref[...]
    w = w_ref[...]

    def head_sum(s):
        r = jnp.maximum(s, 0.0) * w
        out = r[0:t_new]
        for h in range(1, N_HEADS_IDX):
            out = out + r[h * t_new:(h + 1) * t_new]
        return out

    for i in range(pages):
        ikt = page_refs[i][...].astype(BF16)
        sp_ref[:, i * PAGE_SIZE:(i + 1) * PAGE_SIZE] = head_sum(_dot(a, ikt))

    @pl.when(pl.program_id(1) == 0)
    def _():
        sn = head_sum(_dot_nt(a, kin_ref[...]))
        ti = lax.broadcasted_iota(jnp.int32, sn.shape, 0)
        si = lax.broadcasted_iota(jnp.int32, sn.shape, 1)
        sn_ref[...] = jnp.where(si <= ti, sn, -jnp.inf)


def _sample_scores(layer, page_table, a_rows, w_rows, ki_new_pad, cache_idx_kt, n_pool, pages):
    batch, n_pages = page_table.shape
    ht = a_rows.shape[1]
    t_new = ht // N_HEADS_IDX
    page_spec = lambda i: pl.BlockSpec(
        (None, D_IDX, PAGE_SIZE), lambda b, s, pt: (layer * n_pool + pt[b, s * pages + i], 0, 0))
    grid_spec = pltpu.PrefetchScalarGridSpec(
        num_scalar_prefetch=1,
        grid=(batch, n_pages // pages),
        in_specs=[pl.BlockSpec((None, ht, D_IDX), lambda b, s, pt: (b, 0, 0)),
                  pl.BlockSpec((None, ht, LANES), lambda b, s, pt: (b, 0, 0)),
                  pl.BlockSpec((None, LANES, D_IDX), lambda b, s, pt: (b, 0, 0))]
        + [page_spec(i) for i in range(pages)],
        out_specs=[pl.BlockSpec((None, t_new, pages * PAGE_SIZE), lambda b, s, pt: (b, 0, s)),
                   pl.BlockSpec((None, t_new, LANES), lambda b, s, pt: (b, 0, 0))],
    )
    return pl.pallas_call(
        functools.partial(_sample_scores_kernel, pages=pages),
        grid_spec=grid_spec,
        out_shape=[jax.ShapeDtypeStruct((batch, t_new, n_pages * PAGE_SIZE), F32),
                   jax.ShapeDtypeStruct((batch, t_new, LANES), F32)],
        compiler_params=_params("parallel", "arbitrary"),
        name="sample_scores",
    )(page_table, a_rows, w_rows, ki_new_pad, *([cache_idx_kt] * pages))


def _sample_select_kernel(sp_ref, sn_ref, selp_ref, seln_ref, sc_ref, lo_ref, hi_ref, flo_ref, *, width, ktop):
    rows, past = sp_ref.shape
    n_tiles = (past + LANES) // width
    for c in range(n_tiles):
        lo, hi = c * width, (c + 1) * width
        if hi <= past:
            sc_ref[c] = sp_ref[:, lo:hi]
        elif lo == past:
            sc_ref[c] = sn_ref[...]
        else:
            sc_ref[c] = jnp.concatenate([sp_ref[:, lo:past], sn_ref[...]], axis=1)

    def get_tile(c):
        return sc_ref[c]

    def set_tile(c, val):
        sc_ref[c] = val

    thr = _topk_threshold(get_tile, set_tile, n_tiles, rows, width, ktop, lo_ref, hi_ref, flo_ref)
    for c in range(n_tiles):
        lo, hi = c * width, (c + 1) * width
        sel = jnp.concatenate([jnp.where(ch >= thr, 1.0, 0.0) for ch in _chunks(sc_ref[c])], axis=1)
        if hi <= past:
            selp_ref[:, lo:hi] = sel
        elif lo == past:
            seln_ref[...] = sel
        else:
            selp_ref[:, lo:past] = sel[:, 0:past - lo]
            seln_ref[...] = sel[:, past - lo:]


def _sample_select(sp, sn, rows, width, ktop):
    n, past = sp.shape
    spec = lambda w: pl.BlockSpec((rows, w), lambda i: (i, 0))
    return pl.pallas_call(
        functools.partial(_sample_select_kernel, width=width, ktop=ktop),
        grid=(n // rows,),
        in_specs=[spec(past), spec(LANES)],
        out_specs=[spec(past), spec(LANES)],
        out_shape=[jax.ShapeDtypeStruct((n, past), F32), jax.ShapeDtypeStruct((n, LANES), F32)],
        scratch_shapes=[pltpu.VMEM(((past + LANES) // width, rows, width), F32), pltpu.VMEM((rows, LANES), F32),
                        pltpu.VMEM((rows, LANES), F32), pltpu.VMEM((rows, LANES), F32)],
        compiler_params=_params("parallel"),
        name="sample_select",
    )(sp, sn)


def _sample_attn_kernel(pt_ref, q_ref, selp_ref, seln_ref, kn_ref, vn_ref, *rest, pages):
    kt_refs = rest[:pages]
    vt_refs = rest[pages:2 * pages]
    o_ref, m_ref, l_ref, acc_ref = rest[2 * pages:]
    step = pl.program_id(1)
    t_new = o_ref.shape[0]
    q = q_ref[...]

    def tile_rows(sel):
        return jnp.concatenate([sel] * N_HEADS_A, axis=0)

    def update(s, pv):
        m_old = m_ref[...]
        m_new = jnp.maximum(m_old, jnp.max(s, axis=1, keepdims=True))
        alpha = jnp.exp2(m_old - m_new)
        p = jnp.exp2(s - m_new)
        l_ref[...] = alpha * l_ref[...] + jnp.sum(p, axis=1, keepdims=True)
        acc_ref[...] = alpha * acc_ref[...] + pv(p.astype(BF16))
        m_ref[...] = m_new

    @pl.when(step == 0)
    def _():
        m_ref[...] = jnp.full(m_ref.shape, NEG, F32)
        l_ref[...] = jnp.zeros(l_ref.shape, F32)
        acc_ref[...] = jnp.zeros(acc_ref.shape, F32)
        s = jnp.where(tile_rows(seln_ref[...]) > 0.5, _dot_nt(q, kn_ref[...]), NEG)
        update(s, lambda p: _dot(p, vn_ref[...]))

    s = jnp.concatenate([_dot(q, kt_refs[i][...].astype(BF16)) for i in range(pages)], axis=1)
    s = jnp.where(tile_rows(selp_ref[...]) > 0.5, s, NEG)

    def pv(p):
        out = _dot_nt(p[:, 0:PAGE_SIZE], vt_refs[0][...].astype(BF16))
        for i in range(1, pages):
            out = out + _dot_nt(p[:, i * PAGE_SIZE:(i + 1) * PAGE_SIZE], vt_refs[i][...].astype(BF16))
        return out

    update(s, pv)

    @pl.when(step == pl.num_programs(1) - 1)
    def _():
        res = acc_ref[...] / l_ref[...]
        lane_head = lax.broadcasted_iota(jnp.int32, (1, D_ATTN), 1) // HEAD_DIM_A
        out = jnp.zeros((t_new, D_ATTN), F32)
        for h in range(N_HEADS_A):
            out = out + jnp.where(lane_head == h, res[h * t_new:(h + 1) * t_new], 0.0)
        o_ref[...] = out.astype(BF16)


def _sample_attn(layer, page_table, q_rows, selp, seln, kn_pad, vn_pad, cache_kt, cache_vt, n_pool, pages):
    batch, n_pages = page_table.shape
    ht = q_rows.shape[1]
    t_new = ht // N_HEADS_A
    page_spec = lambda i: pl.BlockSpec(
        (None, D_ATTN, PAGE_SIZE), lambda b, s, pt: (layer * n_pool + pt[b, s * pages + i], 0, 0))
    per_b = lambda r, w: pl.BlockSpec((None, r, w), lambda b, s, pt: (b, 0, 0))
    grid_spec = pltpu.PrefetchScalarGridSpec(
        num_scalar_prefetch=1,
        grid=(batch, n_pages // pages),
        in_specs=[per_b(ht, D_ATTN),
                  pl.BlockSpec((None, t_new, pages * PAGE_SIZE), lambda b, s, pt: (b, 0, s)),
                  per_b(t_new, LANES), per_b(LANES, D_ATTN), per_b(LANES, D_ATTN)]
        + [page_spec(i) for i in range(pages)] * 2,
        out_specs=per_b(t_new, D_ATTN),
        scratch_shapes=[pltpu.VMEM((ht, 1), F32), pltpu.VMEM((ht, 1), F32), pltpu.VMEM((ht, D_ATTN), F32)],
    )
    return pl.pallas_call(
        functools.partial(_sample_attn_kernel, pages=pages),
        grid_spec=grid_spec,
        out_shape=jax.ShapeDtypeStruct((batch, t_new, D_ATTN), BF16),
        compiler_params=_params("parallel", "arbitrary"),
        name="sample_attn",
    )(page_table, q_rows, selp, seln, kn_pad, vn_pad, *([cache_kt] * pages), *([cache_vt] * pages))


def _merge_ffn_kernel(x_ref, a_ref, g_ref, ga_ref, gb_ref, wa_ref, wb_ref, wo_ref, n2_ref,
                      wg_ref, wu_ref, wd_ref, nf_ref, y_ref, *, a_key_major, n_split, final_norm):
    a_proj = _dot_tn(a_ref[...], wa_ref[...]) if a_key_major else _dot(a_ref[...], wa_ref[...])
    mixed = (ga_ref[...].astype(F32) * a_proj
             + gb_ref[...].astype(F32) * _dot(g_ref[...], wb_ref[...]))
    x1 = x_ref[...] + _dot(mixed.astype(BF16), wo_ref[...])
    h2 = (x1 * lax.rsqrt(jnp.mean(x1 * x1, axis=-1, keepdims=True) + EPS) * n2_ref[...]).astype(BF16)
    y = x1
    fc = wg_ref.shape[1] // n_split
    for i in range(n_split):
        g = _dot(h2, wg_ref[:, i * fc:(i + 1) * fc])
        u = _dot(h2, wu_ref[:, i * fc:(i + 1) * fc])
        act = (g * jax.nn.sigmoid(g) * u).astype(BF16)
        y = y + _dot(act, wd_ref[i * fc:(i + 1) * fc, :])
    if final_norm:
        y = y * lax.rsqrt(jnp.mean(y * y, axis=-1, keepdims=True) + EPS) * nf_ref[...]
    y_ref[...] = y


def _merge_ffn(x2d, a_o, g_o, ga, gb, w_a, w_b, w_o, norm2, w_g, w_u, w_d, norm_final, batch, tm, a_key_major,
               final_norm):
    n, d = x2d.shape
    nt = n // batch // tm
    d_ff = w_g.shape[1]
    n_split = 2 if d_ff % (2 * LANES) == 0 else 1
    row = lambda w: pl.BlockSpec((tm, w), lambda b, i: (b * nt + i, 0))
    a_spec = pl.BlockSpec((None, D_ATTN, tm), lambda b, i: (b, 0, i)) if a_key_major else row(D_ATTN)
    once = lambda shape: pl.BlockSpec(shape, lambda b, i: (0, 0), pipeline_mode=pl.Buffered(1))
    w1 = (w_a.astype(BF16), w_b.astype(BF16), w_o.astype(BF16))
    w2 = (w_g.astype(BF16), w_u.astype(BF16), w_d.astype(BF16))
    return pl.pallas_call(
        functools.partial(_merge_ffn_kernel, a_key_major=a_key_major, n_split=n_split, final_norm=final_norm),
        grid=(batch, nt),
        in_specs=[row(d), a_spec, row(D_GLA_V), row(d), row(d)]
        + [once(w.shape) for w in w1] + [_const_spec((1, d))]
        + [once(w.shape) for w in w2] + [_const_spec((1, d))],
        out_specs=row(d),
        out_shape=jax.ShapeDtypeStruct((n, d), F32),
        compiler_params=_params("parallel", "parallel"),
        name="merge_ffn",
    )(x2d, a_o, g_o, ga, gb, *w1, norm2.reshape(1, d), *w2, norm_final.reshape(1, d))


def _pick(n, prefs):
    for p in prefs:
        if n % p == 0:
            return p
    return n


def kernel(x_prompt, x_sample, cache_k, cache_v, cache_idx_k, state_gla, page_table, norm1, w_in, w_gate_up,
           b_gate, g_gla_norm, w_branch_a, w_branch_b, w_out, norm2, w_ffn_gate, w_ffn_up, w_ffn_down, norm_final):
    depth = w_in.shape[0]
    bp, seq, d = x_prompt.shape
    bs, t_new, _ = x_sample.shape
    n_pool = cache_k.shape[1]
    n_pages = page_table.shape[1]
    past = n_pages * PAGE_SIZE
    xp = x_prompt.reshape(bp * seq, d)
    xs = x_sample.reshape(bs * t_new, d)
    cache_kt = cache_k.transpose(0, 1, 3, 4, 2).reshape(depth * n_pool, D_ATTN, PAGE_SIZE)
    cache_vt = cache_v.transpose(0, 1, 3, 4, 2).reshape(depth * n_pool, D_ATTN, PAGE_SIZE)
    cache_ikt = cache_idx_k.transpose(0, 1, 3, 2).reshape(depth * n_pool, D_IDX, PAGE_SIZE)
    outs = [[] for _ in range(8)]
    tq = _pick(seq, (256, 128))
    for l in range(depth):
        last = l == depth - 1
        (qkg, vg, la, r, ga, gb, qt, qit, wt, kb, kk, kt, vt, vtb, kit) = _inproj(
            xp, norm1[l], w_in[l], w_gate_up[l], b_gate[l], bp, _pick(seq, (256, 128)), True)
        a_o = _dsa_prompt(qt, qit, wt, kb, vtb, kk, bp, seq, tq)
        chunk = GLA_CHUNK if seq % GLA_CHUNK == 0 else seq
        g_o, st = _gla(qkg, vg, la, r, g_gla_norm[l], jnp.zeros((bp, DV_G, D_GLA_K), F32), bp, seq, chunk,
                       _pick(seq, (512, 256, 128, 64)), _pick(bp, (2, 1)))
        xp = _merge_ffn(xp, a_o, g_o, ga, gb, w_branch_a[l], w_branch_b[l], w_out[l], norm2[l], w_ffn_gate[l],
                        w_ffn_up[l], w_ffn_down[l], norm_final, bp, _pick(seq, (256, 128)), True, last)
        outs[0].append(kt.reshape(bp, N_HEADS_A, HEAD_DIM_A, seq).transpose(0, 3, 1, 2))
        outs[1].append(vt.reshape(bp, N_HEADS_A, HEAD_DIM_A, seq).transpose(0, 3, 1, 2))
        outs[2].append(kit.transpose(0, 2, 1))
        outs[3].append(_state_from_t(st))

        (qkg, vg, la, r, ga, gb, q, qi, kw, k, v, kb, vb, kk) = _inproj(
            xs, norm1[l], w_in[l], w_gate_up[l], b_gate[l], 1, _pick(bs * t_new, (256, 128)), False)
        a_rows = qi.reshape(bs, t_new, N_HEADS_IDX, D_IDX).transpose(0, 2, 1, 3).reshape(bs, N_HEADS_IDX * t_new, D_IDX)
        w_rows = (kw[:, D_IDX:D_IDX + N_HEADS_IDX] * (N_HEADS_IDX ** -0.5 * D_IDX ** -0.5)).reshape(
            bs, t_new, N_HEADS_IDX).transpose(0, 2, 1).reshape(bs, N_HEADS_IDX * t_new, 1)
        w_rows = jnp.broadcast_to(w_rows, (bs, N_HEADS_IDX * t_new, LANES))
        pad_rows = lambda a: jnp.pad(a.reshape(bs, t_new, a.shape[-1]), ((0, 0), (0, LANES - t_new), (0, 0)))
        sp, sn = _sample_scores(l, page_table, a_rows, w_rows, pad_rows(kk[:, 0:D_IDX]), cache_ikt, n_pool,
                                _pick(n_pages, (32, 16, 8, 4, 2, 1)))
        ktop = min(TOPK_MAX, (past + t_new) // 4)
        width = 5 * LANES if (past + LANES) % (5 * LANES) == 0 else LANES
        rows = _pick(bs * t_new, (128, 64, 32, 16, 8))
        selp, seln = _sample_select(sp.reshape(bs * t_new, past), sn.reshape(bs * t_new, LANES), rows, width, ktop)
        head_of_lane = jnp.arange(D_ATTN) // HEAD_DIM_A
        q_rows = jnp.where(head_of_lane[None, None, None, :] == jnp.arange(N_HEADS_A)[None, :, None, None],
                           q.reshape(bs, 1, t_new, D_ATTN), jnp.zeros((), BF16)).reshape(bs, N_HEADS_A * t_new, D_ATTN)
        a_o = _sample_attn(l, page_table, q_rows, selp.reshape(bs, t_new, past), seln.reshape(bs, t_new, LANES),
                           pad_rows(kb), pad_rows(vb), cache_kt, cache_vt, n_pool, _pick(n_pages, (16, 8, 4, 2, 1)))
        gchunk = 16
        pad_g = lambda a: jnp.pad(a.reshape(bs, t_new, a.shape[-1]), ((0, 0), (0, gchunk - t_new), (0, 0))).reshape(
            bs * gchunk, a.shape[-1])
        g_o, st = _gla(pad_g(qkg), pad_g(vg), pad_g(la), pad_g(r), g_gla_norm[l], _state_to_t(state_gla[l].astype(F32)),
                       bs, gchunk, gchunk, gchunk, _pick(bs, (8, 4, 2, 1)))
        g_o = g_o.reshape(bs, gchunk, D_GLA_V)[:, 0:t_new].reshape(bs * t_new, D_GLA_V)
        xs = _merge_ffn(xs, a_o.reshape(bs * t_new, D_ATTN), g_o, ga, gb, w_branch_a[l], w_branch_b[l], w_out[l],
                        norm2[l], w_ffn_gate[l], w_ffn_up[l], w_ffn_down[l], norm_final, 1,
                        _pick(bs * t_new, (256, 128)), False, last)
        outs[4].append(k.reshape(bs, t_new, N_HEADS_A, HEAD_DIM_A))
        outs[5].append(v.reshape(bs, t_new, N_HEADS_A, HEAD_DIM_A))
        outs[6].append(kw[:, 0:D_IDX].reshape(bs, t_new, D_IDX))
        outs[7].append(_state_from_t(st))
    return (xp.reshape(bp, seq, d), xs.reshape(bs, t_new, d)) + tuple(jnp.stack(o) for o in outs)
```
